```python
import jax, jax.numpy as jnp
from jax import lax
import numpy as np

D_MODEL = 1024
BATCH = 8
SEQ = 2048
DEPTH = 1

GRID_W = 64
CTX_LEN = 256
D_RNN = D_MODEL
LRU_BLOCKS = 8
LRU_BLOCK_W = D_RNN // LRU_BLOCKS
LRU_CONV_W = 4
LRU_CONV_LEFT = 2
LRU_C = 8.0
NA_HEADS = 16
HEAD_DIM = 64
D_ATT = NA_HEADS * HEAD_DIM
NA_ROWS = 8
NA_COLS = 16
ROPE_BASE = 10000.0
D_FF = ((8 * D_MODEL // 3 + 127) // 128) * 128
FFN_CONV_W = 3
FFN_CONV_LEFT = 1
N_MOD = 6
EPS = 1e-6
NEG_INF = -1e30
IN_SPLITS = (D_RNN, D_RNN + D_ATT, D_RNN + 2 * D_ATT, 2 * D_RNN + 2 * D_ATT, 2 * D_RNN + 3 * D_ATT, 2 * D_RNN + 3 * D_ATT + D_MODEL)
IN_COLS = 2 * D_RNN + 3 * D_ATT + 2 * D_MODEL
CTX_KV_COLS = D_RNN + 2 * D_ATT

kernel_name = 'hybrid_rglru_natten_dit_block'


def rms_norm(x, g):
    xf = x.astype(jnp.float32)
    y = xf * lax.rsqrt(jnp.mean(xf * xf, axis=-1, keepdims=True) + EPS)
    return (y * g).astype(x.dtype)


def modulate(x, g, shift, scale):
    return rms_norm(x, g) * (1 + scale) + shift


def split_heads(t):
    b, n, _ = t.shape
    return t.reshape(b, n, NA_HEADS, HEAD_DIM)


def head_rms(t, g):
    return rms_norm(split_heads(t), g)


def dwconv(x, w, b, left):
    k_w, ch = w.shape
    y = lax.conv_general_dilated(x, w[:, None, :], window_strides=(1,), padding=[(left, k_w - 1 - left)],
                                 dimension_numbers=('NWC', 'WIO', 'NWC'), feature_group_count=ch)
    return y + b


def _rotate(x, pos):
    n = x.shape[-1] // 2
    freq = ROPE_BASE ** (-jnp.arange(n, dtype=jnp.float32) / n)
    ang = pos.astype(jnp.float32)[:, None] * freq
    cos = jnp.cos(ang)[None, :, None, :]
    sin = jnp.sin(ang)[None, :, None, :]
    x1 = x[..., :n].astype(jnp.float32)
    x2 = x[..., n:].astype(jnp.float32)
    return jnp.concatenate([x1 * cos - x2 * sin, x2 * cos + x1 * sin], axis=-1).astype(x.dtype)


def rope_2d(x, row_pos, col_pos):
    h = x.shape[-1] // 2
    return jnp.concatenate([_rotate(x[..., :h], row_pos), _rotate(x[..., h:], col_pos)], axis=-1)


def rglru_coeffs(xc, wa, ba, wx, bx, lam):
    b_, t_, ch = xc.shape
    xb = xc.reshape(b_, t_, LRU_BLOCKS, LRU_BLOCK_W)
    r = jax.nn.sigmoid((jnp.einsum('btnk,nkj->btnj', xb, wa).reshape(b_, t_, ch) + ba).astype(jnp.float32))
    i = jax.nn.sigmoid((jnp.einsum('btnk,nkj->btnj', xb, wx).reshape(b_, t_, ch) + bx).astype(jnp.float32))
    log_a = -LRU_C * r * jax.nn.softplus(-lam.astype(jnp.float32))
    a = jnp.exp(log_a)
    b = jnp.sqrt(-jnp.expm1(2.0 * log_a)) * i * xc.astype(jnp.float32)
    return a, b


def _lin_combine(left, right):
    a_l, b_l = left
    a_r, b_r = right
    return a_l * a_r, a_r * b_l + b_r


def linear_scan(a, b, h0=None):
    a_cum, h = lax.associative_scan(_lin_combine, (a, b), axis=1)
    if h0 is None:
        return h
    return h + a_cum * h0[:, None, :]


def rglru_bidirectional(x_lat, x_ctx, wa, ba, wx, bx, lam, need_ctx_out):
    lat_seqs, ctx_seqs = [], []
    for d in range(2):
        a_c, b_c = rglru_coeffs(x_ctx, wa[d], ba[d], wx[d], bx[d], lam[d])
        a_x, b_x = rglru_coeffs(x_lat, wa[d], ba[d], wx[d], bx[d], lam[d])
        if d == 1:
            a_c, b_c, a_x, b_x = [jnp.flip(t, axis=1) for t in (a_c, b_c, a_x, b_x)]
        h_c = linear_scan(a_c, b_c)
        h_x = linear_scan(a_x, b_x, h_c[:, -1])
        if d == 1:
            h_x = jnp.flip(h_x, axis=1)
        lat_seqs.append(h_x)
        if need_ctx_out:
            ctx_seqs.append(jnp.flip(h_c, axis=1) if d == 1 else h_c)
    h_lat = (lat_seqs[0] + lat_seqs[1]).astype(x_lat.dtype)
    h_ctx = (ctx_seqs[0] + ctx_seqs[1]).astype(x_ctx.dtype) if need_ctx_out else None
    return h_lat, h_ctx


def neighborhood_attention(q_rot, q_plain, k_rot, v, k_ctx, v_ctx, rpb):
    b_, t_, h_, dh = q_rot.shape
    rows = t_ // GRID_W
    kh = min(NA_ROWS, rows)
    r = jnp.arange(rows)
    row_start = jnp.clip(r - kh // 2, 0, rows - kh)
    row_idx = row_start[:, None] + jnp.arange(kh)[None, :]
    cidx = jnp.arange(GRID_W)
    col_start = jnp.clip(cidx - NA_COLS // 2, 0, GRID_W - NA_COLS)
    in_win = (cidx[None, :] >= col_start[:, None]) & (cidx[None, :] < col_start[:, None] + NA_COLS)
    dr = row_idx - r[:, None] + (NA_ROWS - 1)
    dc = jnp.clip(cidx[None, :] - cidx[:, None], -(NA_COLS - 1), NA_COLS - 1) + (NA_COLS - 1)
    bias = rpb[:, dr[:, None, :, None], dc[None, :, None, :]]
    bias = jnp.where(in_win[None, None, :, None, :], bias, NEG_INF)
    qg = q_rot.reshape(b_, rows, GRID_W, h_, dh)
    kb = k_rot.reshape(b_, rows, GRID_W, h_, dh)[:, row_idx]
    vb = v.reshape(b_, rows, GRID_W, h_, dh)[:, row_idx]
    scale = dh ** -0.5
    s_lat = jnp.einsum('brchd,brikhd->bhrcik', qg, kb).astype(jnp.float32) * scale + bias[None]
    s_ctx = jnp.einsum('brchd,bnhd->bhrcn', q_plain.reshape(b_, rows, GRID_W, h_, dh), k_ctx).astype(jnp.float32) * scale
    n_lat = kh * GRID_W
    s = jnp.concatenate([s_lat.reshape(b_, h_, rows, GRID_W, n_lat), s_ctx], axis=-1)
    p = jax.nn.softmax(s, axis=-1)
    p_lat = p[..., :n_lat].reshape(b_, h_, rows, GRID_W, kh, GRID_W).astype(v.dtype)
    p_ctx = p[..., n_lat:].astype(v.dtype)
    o = jnp.einsum('bhrcik,brikhd->brchd', p_lat, vb) + jnp.einsum('bhrcn,bnhd->brchd', p_ctx, v_ctx)
    return o.reshape(b_, t_, h_ * dh)


def context_attention(q, k, v):
    b_, n_, h_, dh = q.shape
    s = jnp.einsum('bnhd,bmhd->bhnm', q, k).astype(jnp.float32) * (dh ** -0.5)
    p = jax.nn.softmax(s, axis=-1).astype(v.dtype)
    return jnp.einsum('bhnm,bmhd->bnhd', p, v).reshape(b_, n_, h_ * dh)


def merge_branches(y_rnn, y_na, g_rnn, g_na, w_rnn_out, w_na_out, w_out):
    merged = jax.nn.sigmoid(g_rnn) * (y_rnn @ w_rnn_out) + jax.nn.sigmoid(g_na) * (y_na @ w_na_out)
    return merged @ w_out


def conv_ffn(xn, w_up, conv_w, conv_b, w_down):
    h = dwconv(xn @ w_up, conv_w, conv_b, FFN_CONV_LEFT)
    a, g = jnp.split(h, 2, axis=-1)
    return (jax.nn.silu(a) * g) @ w_down


def setup_inputs(seed: int = 0) -> dict:
    key = jax.random.key(seed)
    ks = jax.random.split(key, 32)
    L = DEPTH

    def nrm(k, shape, scale):
        return jax.random.normal(k, shape, jnp.float32) * scale

    a0 = jax.random.uniform(ks[15], (L, 2, D_RNN), jnp.float32, 0.9, 0.999)
    return {
        'x': nrm(ks[0], (BATCH, SEQ, D_MODEL), 1.0),
        'c': nrm(ks[1], (BATCH, D_MODEL), 1.0),
        'ctx': nrm(ks[2], (BATCH, CTX_LEN, D_MODEL), 1.0),
        'c_ctx': nrm(ks[3], (D_MODEL,), 1.0),
        'w_mod': nrm(ks[4], (L, D_MODEL, N_MOD * D_MODEL), 0.5 * D_MODEL ** -0.5),
        'b_mod': nrm(ks[5], (L, N_MOD * D_MODEL), 0.02),
        'norm_mix_g': 1.0 + nrm(ks[6], (L, D_MODEL), 0.02),
        'norm_ffn_g': 1.0 + nrm(ks[7], (L, D_MODEL), 0.02),
        'w_in': nrm(ks[8], (L, D_MODEL, IN_COLS), D_MODEL ** -0.5),
        'lru_conv_w': nrm(ks[9], (L, LRU_CONV_W, D_RNN), LRU_CONV_W ** -0.5),
        'lru_conv_b': nrm(ks[10], (L, D_RNN), 0.02),
        'lru_wa': nrm(ks[11], (L, 2, LRU_BLOCKS, LRU_BLOCK_W, LRU_BLOCK_W), LRU_BLOCK_W ** -0.5),
        'lru_ba': nrm(ks[12], (L, 2, D_RNN), 0.02),
        'lru_wx': nrm(ks[13], (L, 2, LRU_BLOCKS, LRU_BLOCK_W, LRU_BLOCK_W), LRU_BLOCK_W ** -0.5),
        'lru_bx': nrm(ks[14], (L, 2, D_RNN), 0.02),
        'lru_lambda': jnp.log(a0) - jnp.log1p(-a0),
        'q_norm_g': 1.0 + nrm(ks[16], (L, HEAD_DIM), 0.02),
        'k_norm_g': 1.0 + nrm(ks[17], (L, HEAD_DIM), 0.02),
        'na_rpb': nrm(ks[18], (L, NA_HEADS, 2 * NA_ROWS - 1, 2 * NA_COLS - 1), 0.1),
        'w_rnn_out': nrm(ks[19], (L, D_RNN, D_MODEL), D_RNN ** -0.5),
        'w_na_out': nrm(ks[20], (L, D_ATT, D_MODEL), D_ATT ** -0.5),
        'w_out': nrm(ks[21], (L, D_MODEL, D_MODEL), D_MODEL ** -0.5),
        'w_up': nrm(ks[22], (L, D_MODEL, 2 * D_FF), D_MODEL ** -0.5),
        'ffn_conv_w': nrm(ks[23], (L, FFN_CONV_W, 2 * D_FF), FFN_CONV_W ** -0.5),
        'ffn_conv_b': nrm(ks[24], (L, 2 * D_FF), 0.02),
        'w_down': nrm(ks[25], (L, D_FF, D_MODEL), D_FF ** -0.5),
    }


def reference(x, c, ctx, c_ctx, w_mod, b_mod, norm_mix_g, norm_ffn_g, w_in, lru_conv_w, lru_conv_b,
              lru_wa, lru_ba, lru_wx, lru_bx, lru_lambda, q_norm_g, k_norm_g, na_rpb,
              w_rnn_out, w_na_out, w_out, w_up, ffn_conv_w, ffn_conv_b, w_down):
    seq = x.shape[1]
    t = jnp.arange(seq)
    row_pos = t // GRID_W
    col_pos = t % GRID_W
    s_c = jax.nn.silu(c)
    s_ctx = jax.nn.silu(c_ctx)
    for l in range(DEPTH):
        last = l == DEPTH - 1
        mx = jnp.split((s_c @ w_mod[l] + b_mod[l])[:, None, :], N_MOD, axis=-1)
        mc = jnp.split((s_ctx @ w_mod[l] + b_mod[l])[None, None, :], N_MOD, axis=-1)
        xn = modulate(x, norm_mix_g[l], mx[0], mx[1])
        cn = modulate(ctx, norm_mix_g[l], mc[0], mc[1])
        xr, kx, vx, gx, qx, mrx, mnx = jnp.split(xn @ w_in[l], list(IN_SPLITS), axis=-1)
        if last:
            cr, kc, vc = jnp.split(cn @ w_in[l][:, :CTX_KV_COLS], [D_RNN, D_RNN + D_ATT], axis=-1)
        else:
            cr, kc, vc, gc, qc, mrc, mnc = jnp.split(cn @ w_in[l], list(IN_SPLITS), axis=-1)
        xr = dwconv(xr, lru_conv_w[l], lru_conv_b[l], LRU_CONV_LEFT)
        cr = dwconv(cr, lru_conv_w[l], lru_conv_b[l], LRU_CONV_LEFT)
        h_x, h_c = rglru_bidirectional(xr, cr, lru_wa[l], lru_ba[l], lru_wx[l], lru_bx[l], lru_lambda[l], not last)
        y_rnn_x = h_x * jax.nn.gelu(gx)
        qx = head_rms(qx, q_norm_g[l])
        kx = head_rms(kx, k_norm_g[l])
        kc = head_rms(kc, k_norm_g[l])
        vx = split_heads(vx)
        vc = split_heads(vc)
        y_na_x = neighborhood_attention(rope_2d(qx, row_pos, col_pos), qx, rope_2d(kx, row_pos, col_pos),
                                        vx, kc, vc, na_rpb[l])
        x = x + mx[2] * merge_branches(y_rnn_x, y_na_x, mrx, mnx, w_rnn_out[l], w_na_out[l], w_out[l])
        x = x + mx[5] * conv_ffn(modulate(x, norm_ffn_g[l], mx[3], mx[4]), w_up[l], ffn_conv_w[l], ffn_conv_b[l], w_down[l])
        if not last:
            y_rnn_c = h_c * jax.nn.gelu(gc)
            y_na_c = context_attention(head_rms(qc, q_norm_g[l]), kc, vc)
            ctx = ctx + mc[2] * merge_branches(y_rnn_c, y_na_c, mrc, mnc, w_rnn_out[l], w_na_out[l], w_out[l])
            ctx = ctx + mc[5] * conv_ffn(modulate(ctx, norm_ffn_g[l], mc[3], mc[4]), w_up[l], ffn_conv_w[l], ffn_conv_b[l], w_down[l])
    return x
```

```python
import functools

import jax
import jax.numpy as jnp
from jax import lax
from jax.experimental import pallas as pl
from jax.experimental.pallas import tpu as pltpu

F32 = jnp.float32
BF16 = jnp.bfloat16

D_MODEL = 1024
GRID_W = 64
D_RNN = D_MODEL
LRU_BLOCKS = 8
LRU_BLOCK_W = D_RNN // LRU_BLOCKS
LRU_CONV_W = 4
LRU_CONV_LEFT = 2
LRU_C = 8.0
NA_HEADS = 16
HEAD_DIM = 64
D_ATT = NA_HEADS * HEAD_DIM
NA_ROWS = 8
NA_COLS = 16
ROPE_BASE = 10000.0
D_FF = ((8 * D_MODEL // 3 + 127) // 128) * 128
FFN_CONV_W = 3
FFN_CONV_LEFT = 1
N_MOD = 6
EPS = 1e-6
NEG_INF = -1e30

COL_XR, COL_K, COL_V, COL_GX, COL_Q, COL_MR, COL_MN = range(7)

LANES = 128
SUBLANES = 8
MXU_DIM = 256
VMEM_LIMIT_CAP = 60000 * 1024
VMEM_TEMP_HEADROOM = 12 * 1024 * 1024

MOD_ROWS = 16
TOK_TILE = 512
LRU_CG = 256
LRU_CHUNK = 128
HALO = SUBLANES
FF_CHUNK = MXU_DIM
N_BIAS_VARIANTS = 8


def _nbytes(shape, dtype):
    n = 1
    for s in shape:
        n *= s
    return n * jnp.dtype(dtype).itemsize


def _vmem_limit(buffers):
    need = sum(_nbytes(s, d) * c for s, d, c in buffers) + VMEM_TEMP_HEADROOM
    return int(min(need, VMEM_LIMIT_CAP))


def _params(semantics, buffers):
    return pltpu.CompilerParams(dimension_semantics=semantics,
                                vmem_limit_bytes=_vmem_limit(buffers))


def _modulate(x, g, shift, scale):
    ms = jnp.mean(x * x, axis=-1, keepdims=True)
    y = x * lax.rsqrt(ms + EPS)
    return (y * g) * (1.0 + scale) + shift


def _mod_kernel(cc_ref, w_ref, b_ref, o_ref):
    s = cc_ref[...]
    s = s * jax.nn.sigmoid(s)
    o_ref[...] = jnp.dot(s.astype(BF16), w_ref[...].astype(BF16),
                         preferred_element_type=F32) + b_ref[...]


def _mod_call(cc, w_mod, b_mod):
    n = w_mod.shape[1]
    tn = D_MODEL
    bufs = [((MOD_ROWS, D_MODEL), F32, 2), ((D_MODEL, tn), F32, 2), ((MOD_ROWS, tn), F32, 2)]
    return pl.pallas_call(
        _mod_kernel,
        grid=(n // tn,),
        in_specs=[pl.BlockSpec((MOD_ROWS, D_MODEL), lambda j: (0, 0)),
                  pl.BlockSpec((D_MODEL, tn), lambda j: (0, j)),
                  pl.BlockSpec((1, tn), lambda j: (0, j))],
        out_specs=pl.BlockSpec((MOD_ROWS, tn), lambda j: (0, j)),
        out_shape=jax.ShapeDtypeStruct((MOD_ROWS, n), F32),
        compiler_params=_params(("arbitrary",), bufs),
        name="adaln_mod",
    )(cc, w_mod, b_mod)


def _head_rms(t, gain, e):
    sq = t * t
    hi = sq.astype(BF16)
    lo = (sq - hi.astype(F32)).astype(BF16)
    ms = jnp.dot(hi, e, preferred_element_type=F32) + jnp.dot(lo, e, preferred_element_type=F32)
    return (t * lax.rsqrt(ms + EPS)) * gain


def _rope(y, cos, sin_signed, first_half):
    partner = jnp.where(first_half, pltpu.roll(y, LANES - 16, 1), pltpu.roll(y, 16, 1))
    return y * cos + partner * sin_signed


def _qkv_kernel(*refs, latent):
    if latent:
        (x_ref, sh_ref, sc_ref, g_ref, wk_ref, wv_ref, wq_ref, kg_ref, qg_ref, e_ref,
         cos_ref, sin_ref, k_out, v_out, qr_out, qp_out) = refs
    else:
        (x_ref, sh_ref, sc_ref, g_ref, wk_ref, wv_ref, kg_ref, e_ref, k_out, v_out) = refs
    xn = _modulate(x_ref[...], g_ref[...], sh_ref[0], sc_ref[0]).astype(BF16)
    e = e_ref[...]
    if latent:
        cos = cos_ref[...]
        sin = sin_ref[...]
        lane = lax.broadcasted_iota(jnp.int32, cos.shape, 1)
        first_half = (lane % 32) < 16
    scale = HEAD_DIM ** -0.5
    for j in range(D_ATT // MXU_DIM):
        sl = slice(MXU_DIM * j, MXU_DIM * (j + 1))
        v_out[:, sl] = jnp.dot(xn, wv_ref[:, sl], preferred_element_type=F32).astype(BF16)
        k = _head_rms(jnp.dot(xn, wk_ref[:, sl], preferred_element_type=F32), kg_ref[:, sl], e)
        if not latent:
            k_out[:, sl] = k.astype(BF16)
            continue
        q = _head_rms(jnp.dot(xn, wq_ref[:, sl], preferred_element_type=F32), qg_ref[:, sl], e)
        qp_out[:, sl] = (q * scale).astype(BF16)
        for h in range(MXU_DIM // LANES):
            hl = slice(LANES * h, LANES * (h + 1))
            ol = slice(MXU_DIM * j + LANES * h, MXU_DIM * j + LANES * (h + 1))
            k_out[:, ol] = _rope(k[:, hl], cos, sin, first_half).astype(BF16)
            qr_out[:, ol] = (_rope(q[:, hl], cos, sin, first_half) * scale).astype(BF16)


def _qkv_call(x, mods3, mod_row0, g_mix, w_in, k_gain, q_gain, e_mat, cos_t, sin_t, latent):
    b, t, d = x.shape
    tm = min(TOK_TILE, t)
    nt = t // tm
    wblk = lambda c: pl.BlockSpec((D_MODEL, D_MODEL), lambda bi, i: (0, c))
    vec = pl.BlockSpec((1, D_MODEL), lambda bi, i: (0, 0))
    if latent:
        mod = lambda k: pl.BlockSpec((1, 1, D_MODEL), lambda bi, i: (bi * N_MOD + k, 0, 0))
    else:
        mod = lambda k: pl.BlockSpec((1, 1, D_MODEL), lambda bi, i: (mod_row0 * N_MOD + k, 0, 0))
    tok = pl.BlockSpec((None, tm, d), lambda bi, i: (bi, i, 0))
    in_specs = [tok, mod(0), mod(1), vec, wblk(COL_K), wblk(COL_V)]
    args = [x, mods3, mods3, g_mix, w_in, w_in]
    if latent:
        in_specs += [wblk(COL_Q), vec, vec]
        args += [w_in, k_gain, q_gain]
    else:
        in_specs += [vec]
        args += [k_gain]
    in_specs += [pl.BlockSpec((MXU_DIM, MXU_DIM), lambda bi, i: (0, 0))]
    args += [e_mat]
    n_out = 2
    if latent:
        rope_spec = pl.BlockSpec((tm, LANES), lambda bi, i: (i, 0))
        in_specs += [rope_spec, rope_spec]
        args += [cos_t, sin_t]
        n_out = 4
    out_sd = jax.ShapeDtypeStruct((b, t, D_ATT), BF16)
    bufs = [((tm, d), F32, 2), ((D_MODEL, D_MODEL), BF16, 6), ((tm, D_ATT), BF16, 2 * n_out),
            ((tm, LANES), F32, 4), ((tm, d), BF16, 1)]
    return pl.pallas_call(
        functools.partial(_qkv_kernel, latent=latent),
        grid=(b, nt),
        in_specs=in_specs,
        out_specs=[tok] * n_out,
        out_shape=[out_sd] * n_out,
        compiler_params=_params(("arbitrary", "arbitrary"), bufs),
        name="qkv_proj" if latent else "ctx_proj",
    )(*args)


def _softplus(x):
    return jnp.maximum(x, 0.0) + jnp.log1p(jnp.exp(-jnp.abs(x)))


def _lru_coeffs(src_ref, row0, n, conv_w, conv_b, wcat_ref, bcat, sp, d):
    xe = src_ref[pl.ds(row0, n + 2 * HALO), :]
    xc = conv_b
    for k in range(LRU_CONV_W):
        off = HALO - LRU_CONV_LEFT + k
        xc = xc + conv_w[k:k + 1, :] * xe[off:off + n]
    xcb = xc.astype(BF16)
    rs, gs = [], []
    for blk in range(LRU_CG // LRU_BLOCK_W):
        bl = slice(LRU_BLOCK_W * blk, LRU_BLOCK_W * (blk + 1))
        z = jnp.dot(xcb[:, bl], wcat_ref[d, blk], preferred_element_type=F32)
        rs.append(z[:, :LRU_BLOCK_W])
        gs.append(z[:, LRU_BLOCK_W:])
    r = jax.nn.sigmoid(jnp.concatenate(rs, axis=-1) + bcat[2 * d:2 * d + 1, :])
    gate = jax.nn.sigmoid(jnp.concatenate(gs, axis=-1) + bcat[2 * d + 1:2 * d + 2, :])
    log_a = (-LRU_C * r) * sp[d:d + 1, :]
    a = jnp.exp(log_a)
    b = jnp.sqrt(-jnp.tanh(log_a) * (a * a + 1.0)) * gate * xc
    return a, b


def _scan_chunk(a, b, carry, reverse):
    n, c = a.shape
    ng = n // SUBLANES
    av = a.reshape(ng, SUBLANES, c)
    bv = b.reshape(ng, SUBLANES, c)
    rid = lax.broadcasted_iota(jnp.int32, (ng, SUBLANES, c), 1)
    for s in (1, 2, 4):
        shift = SUBLANES - s if reverse else s
        m = (rid < SUBLANES - s) if reverse else (rid >= s)
        a_nb = pltpu.roll(av, shift, 1)
        b_nb = pltpu.roll(bv, shift, 1)
        bv = jnp.where(m, av * b_nb + bv, bv)
        av = jnp.where(m, av * a_nb, av)
    edge = 0 if reverse else SUBLANES - 1
    order = range(ng - 1, -1, -1) if reverse else range(ng)
    hs = [None] * ng
    for g in order:
        hs[g] = bv[g] + av[g] * carry
        carry = av[g, edge:edge + 1, :] * carry + bv[g, edge:edge + 1, :]
    return jnp.concatenate(hs, axis=0), carry


def _lru_kernel(x_ref, ctx_ref, shx_ref, scx_ref, shc_ref, scc_ref, g_ref, wxr_ref, wgx_ref,
                cw_ref, cb_ref, wcat_ref, bcat_ref, lam_ref, y_ref,
                xn_s, cn_s, xr_s, cr_s, gg_s, hf_s):
    t = x_ref.shape[0]
    nctx = ctx_ref.shape[0]
    cg = y_ref.shape[1]

    @pl.when(pl.program_id(1) == 0)
    def _():
        def body(i, _):
            r0 = pl.multiple_of(i * TOK_TILE, TOK_TILE)
            xn_s[pl.ds(r0, TOK_TILE), :] = _modulate(
                x_ref[pl.ds(r0, TOK_TILE), :], g_ref[...], shx_ref[0], scx_ref[0]).astype(BF16)
            return 0
        lax.fori_loop(0, t // TOK_TILE, body, 0)
        cn_s[...] = _modulate(ctx_ref[...], g_ref[...], shc_ref[0], scc_ref[0]).astype(BF16)

    zero_halo = jnp.zeros((HALO, cg), F32)
    xr_s[0:HALO, :] = zero_halo
    xr_s[t + HALO:t + 2 * HALO, :] = zero_halo
    cr_s[0:HALO, :] = zero_halo
    cr_s[nctx + HALO:nctx + 2 * HALO, :] = zero_halo

    def proj(i, _):
        r0 = pl.multiple_of(i * TOK_TILE, TOK_TILE)
        xn = xn_s[pl.ds(r0, TOK_TILE), :]
        xr_s[pl.ds(r0 + HALO, TOK_TILE), :] = jnp.dot(xn, wxr_ref[...], preferred_element_type=F32)
        gg_s[pl.ds(r0, TOK_TILE), :] = jax.nn.gelu(
            jnp.dot(xn, wgx_ref[...], preferred_element_type=F32))
        return 0
    lax.fori_loop(0, t // TOK_TILE, proj, 0)
    cr_s[HALO:HALO + nctx, :] = jnp.dot(cn_s[...], wxr_ref[...], preferred_element_type=F32)

    conv_w = cw_ref[...]
    conv_b = cb_ref[...]
    bcat = bcat_ref[...]
    sp = _softplus(-lam_ref[...])
    coeffs = functools.partial(_lru_coeffs, n=LRU_CHUNK, conv_w=conv_w, conv_b=conv_b,
                               wcat_ref=wcat_ref, bcat=bcat, sp=sp)
    n_lat = t // LRU_CHUNK
    n_ctx = nctx // LRU_CHUNK

    carry = jnp.zeros((1, cg), F32)
    for i in range(n_ctx):
        a, b = coeffs(cr_s, i * LRU_CHUNK, d=0)
        _, carry = _scan_chunk(a, b, carry, reverse=False)

    def fwd(i, carry):
        r0 = pl.multiple_of(i * LRU_CHUNK, LRU_CHUNK)
        a, b = coeffs(xr_s, r0, d=0)
        h, carry = _scan_chunk(a, b, carry, reverse=False)
        hf_s[pl.ds(r0, LRU_CHUNK), :] = h
        return carry
    lax.fori_loop(0, n_lat, fwd, carry)

    carry = jnp.zeros((1, cg), F32)
    for i in range(n_ctx - 1, -1, -1):
        a, b = coeffs(cr_s, i * LRU_CHUNK, d=1)
        _, carry = _scan_chunk(a, b, carry, reverse=True)

    def bwd(j, carry):
        r0 = pl.multiple_of((n_lat - 1 - j) * LRU_CHUNK, LRU_CHUNK)
        a, b = coeffs(xr_s, r0, d=1)
        h, carry = _scan_chunk(a, b, carry, reverse=True)
        rows = pl.ds(r0, LRU_CHUNK)
        y_ref[rows, :] = ((hf_s[rows, :] + h) * gg_s[rows, :]).astype(BF16)
        return carry
    lax.fori_loop(0, n_lat, bwd, carry)


def _lru_call(x, ctx, mods3, g_mix, w_in, conv_w, conv_b, wcat, bcat, lam):
    b, t, d = x.shape
    nctx = ctx.shape[1]
    cg = LRU_CG
    ng = D_RNN // cg
    ctx_row = b * N_MOD
    mod = lambda k: pl.BlockSpec((1, 1, D_MODEL), lambda bi, g: (bi * N_MOD + k, 0, 0))
    modc = lambda k: pl.BlockSpec((1, 1, D_MODEL), lambda bi, g: (ctx_row + k, 0, 0))
    in_specs = [
        pl.BlockSpec((None, t, d), lambda bi, g: (bi, 0, 0)),
        pl.BlockSpec((None, nctx, d), lambda bi, g: (bi, 0, 0)),
        mod(0), mod(1), modc(0), modc(1),
        pl.BlockSpec((1, D_MODEL), lambda bi, g: (0, 0)),
        pl.BlockSpec((D_MODEL, cg), lambda bi, g: (0, COL_XR * ng + g)),
        pl.BlockSpec((D_MODEL, cg), lambda bi, g: (0, COL_GX * ng + g)),
        pl.BlockSpec((LRU_CONV_W, cg), lambda bi, g: (0, g)),
        pl.BlockSpec((1, cg), lambda bi, g: (0, g)),
        pl.BlockSpec((2, cg // LRU_BLOCK_W, LRU_BLOCK_W, 2 * LRU_BLOCK_W), lambda bi, g: (0, g, 0, 0)),
        pl.BlockSpec((4, cg), lambda bi, g: (0, g)),
        pl.BlockSpec((2, cg), lambda bi, g: (0, g)),
    ]
    scratch = [
        pltpu.VMEM((t, d), BF16), pltpu.VMEM((nctx, d), BF16),
        pltpu.VMEM((t + 2 * HALO, cg), F32), pltpu.VMEM((nctx + 2 * HALO, cg), F32),
        pltpu.VMEM((t, cg), F32), pltpu.VMEM((t, cg), F32),
    ]
    bufs = [((t, d), F32, 2), ((nctx, d), F32, 2), ((D_MODEL, cg), BF16, 4), ((t, cg), BF16, 2),
            ((t, d), BF16, 1), ((nctx, d), BF16, 1), ((t + 2 * HALO, cg), F32, 3),
            ((nctx + 2 * HALO, cg), F32, 1)]
    return pl.pallas_call(
        _lru_kernel,
        grid=(b, ng),
        in_specs=in_specs,
        out_specs=pl.BlockSpec((None, t, cg), lambda bi, g: (bi, 0, g)),
        out_shape=jax.ShapeDtypeStruct((b, t, D_RNN), BF16),
        scratch_shapes=scratch,
        compiler_params=_params(("arbitrary", "arbitrary"), bufs),
        name="rglru",
    )(x, ctx, mods3, mods3, mods3, mods3, g_mix, w_in, w_in, conv_w, conv_b, wcat, bcat, lam)


NT_DIMS = (((1,), (1,)), ((), ()))


def _attn_kernel(qr_ref, qp_ref, k_ref, v_ref, kc_ref, vc_ref, bias_ref, y_ref, qs_s, qps_s):
    t = k_ref.shape[0]
    rows = t // GRID_W
    kh = NA_ROWS
    lane = lax.broadcasted_iota(jnp.int32, (rows, GRID_W, LANES), 2)
    head_a = lane < HEAD_DIM

    def stack_heads(q_ref, dst):
        q3 = q_ref[...].reshape(rows, GRID_W, LANES)
        zero = jnp.zeros_like(q3)
        dst[...] = jnp.concatenate([jnp.where(head_a, q3, zero), jnp.where(head_a, zero, q3)], axis=1)
    stack_heads(qr_ref, qs_s)
    stack_heads(qp_ref, qps_s)

    kc = kc_ref[...]
    vc = vc_ref[...]
    out_a = lax.broadcasted_iota(jnp.int32, (GRID_W, LANES), 1) < HEAD_DIM

    def body(r, _):
        rs = jnp.clip(r - kh // 2, 0, rows - kh)
        var = jnp.where(r < kh // 2, r, jnp.where(r > rows - kh // 2, r - (rows - kh), kh // 2))
        k0 = pl.multiple_of(rs * GRID_W, GRID_W)
        k_win = k_ref[pl.ds(k0, kh * GRID_W), :]
        v_win = v_ref[pl.ds(k0, kh * GRID_W), :]
        s_lat = lax.dot_general(qs_s[r], k_win, NT_DIMS, preferred_element_type=F32) + bias_ref[var]
        s_ctx = lax.dot_general(qps_s[r], kc, NT_DIMS, preferred_element_type=F32)
        m = jnp.maximum(jnp.max(s_lat, axis=-1, keepdims=True), jnp.max(s_ctx, axis=-1, keepdims=True))
        p_lat = jnp.exp(s_lat - m)
        p_ctx = jnp.exp(s_ctx - m)
        denom = jnp.sum(p_lat, axis=-1, keepdims=True) + jnp.sum(p_ctx, axis=-1, keepdims=True)
        o = (jnp.dot(p_lat.astype(BF16), v_win, preferred_element_type=F32)
             + jnp.dot(p_ctx.astype(BF16), vc, preferred_element_type=F32)) / denom
        q0 = pl.multiple_of(r * GRID_W, GRID_W)
        y_ref[pl.ds(q0, GRID_W), :] = jnp.where(out_a, o[:GRID_W], o[GRID_W:]).astype(BF16)
        return 0
    lax.fori_loop(0, rows, body, 0)


def _attn_call(q_rot, q_plain, k_rot, v, kc, vc, bias_tab):
    b, t, _ = q_rot.shape
    nctx = kc.shape[1]
    rows = t // GRID_W
    npair = D_ATT // LANES
    tok = pl.BlockSpec((None, t, LANES), lambda bi, hp: (bi, 0, hp))
    ctx = pl.BlockSpec((None, nctx, LANES), lambda bi, hp: (bi, 0, hp))
    bias_shape = (N_BIAS_VARIANTS, 2 * GRID_W, NA_ROWS * GRID_W)
    bufs = [((t, LANES), BF16, 10), ((nctx, LANES), BF16, 4), (bias_shape, F32, 2),
            ((rows, 2 * GRID_W, LANES), BF16, 2)]
    return pl.pallas_call(
        _attn_kernel,
        grid=(b, npair),
        in_specs=[tok, tok, tok, tok, ctx, ctx,
                  pl.BlockSpec((None,) + bias_shape, lambda bi, hp: (hp, 0, 0, 0))],
        out_specs=tok,
        out_shape=jax.ShapeDtypeStruct((b, t, D_ATT), BF16),
        scratch_shapes=[pltpu.VMEM((rows, 2 * GRID_W, LANES), BF16),
                        pltpu.VMEM((rows, 2 * GRID_W, LANES), BF16)],
        compiler_params=_params(("arbitrary", "arbitrary"), bufs),
        name="nbr_attn",
    )(q_rot, q_plain, k_rot, v, kc, vc, bias_tab)


def _bias_table(rpb, rows):
    kh = NA_ROWS
    c = jnp.arange(GRID_W)
    col_start = jnp.clip(c - NA_COLS // 2, 0, GRID_W - NA_COLS)
    in_win = (c[None, :] >= col_start[:, None]) & (c[None, :] < col_start[:, None] + NA_COLS)
    dc = jnp.clip(c[None, :] - c[:, None], -(NA_COLS - 1), NA_COLS - 1) + (NA_COLS - 1)
    toe = jnp.where(in_win[None, None], rpb[:, :, dc], NEG_INF)
    variant_rows = list(range(kh // 2)) + [kh // 2] + list(range(rows - kh // 2 + 1, rows))
    blocks = []
    for r in variant_rows:
        rs = min(max(r - kh // 2, 0), rows - kh)
        dr = rs + jnp.arange(kh) - r + (NA_ROWS - 1)
        blk = toe[:, dr]
        blocks.append(blk.transpose(0, 2, 1, 3).reshape(NA_HEADS, GRID_W, kh * GRID_W))
    tab = jnp.stack(blocks, axis=1)
    tab = tab.reshape(NA_HEADS // 2, 2, len(variant_rows), GRID_W, kh * GRID_W)
    return tab.transpose(0, 2, 1, 3, 4).reshape(NA_HEADS // 2, len(variant_rows), 2 * GRID_W, kh * GRID_W)


def _merge_kernel(x_ref, yr_ref, yn_ref, sh_ref, sc_ref, gate_ref, g_ref,
                  wgr_ref, wgn_ref, wr_ref, wn_ref, wo_ref, o_ref):
    x = x_ref[...]
    xn = _modulate(x, g_ref[...], sh_ref[0], sc_ref[0]).astype(BF16)
    z_r = jnp.dot(yr_ref[...], wr_ref[...], preferred_element_type=F32)
    z_n = jnp.dot(yn_ref[...], wn_ref[...], preferred_element_type=F32)
    g_r = jax.nn.sigmoid(jnp.dot(xn, wgr_ref[...], preferred_element_type=F32))
    g_n = jax.nn.sigmoid(jnp.dot(xn, wgn_ref[...], preferred_element_type=F32))
    merged = g_r * z_r + g_n * z_n
    o_ref[...] = x + gate_ref[0] * jnp.dot(merged.astype(BF16), wo_ref[...], preferred_element_type=F32)


def _merge_call(x, y_rnn, y_na, mods3, g_mix, w_in, w_rnn_out, w_na_out, w_out):
    b, t, d = x.shape
    tm = TOK_TILE
    tok = pl.BlockSpec((None, tm, d), lambda bi, i: (bi, i, 0))
    mod = lambda k: pl.BlockSpec((1, 1, D_MODEL), lambda bi, i: (bi * N_MOD + k, 0, 0))
    wblk = lambda c: pl.BlockSpec((D_MODEL, D_MODEL), lambda bi, i: (0, c))
    bufs = [((tm, d), F32, 4), ((tm, d), BF16, 4), ((D_MODEL, D_MODEL), BF16, 10), ((tm, d), F32, 4)]
    return pl.pallas_call(
        _merge_kernel,
        grid=(b, t // tm),
        in_specs=[tok, tok, tok, mod(0), mod(1), mod(2),
                  pl.BlockSpec((1, D_MODEL), lambda bi, i: (0, 0)),
                  wblk(COL_MR), wblk(COL_MN), wblk(0), wblk(0), wblk(0)],
        out_specs=tok,
        out_shape=jax.ShapeDtypeStruct((b, t, d), F32),
        compiler_params=_params(("arbitrary", "arbitrary"), bufs),
        name="merge",
    )(x, y_rnn, y_na, mods3, mods3, mods3, g_mix, w_in, w_in, w_rnn_out, w_na_out, w_out)


def _ffn_kernel(x_ref, xp_ref, xnx_ref, sh_ref, sc_ref, gate_ref, g_ref, wup_ref, cw_ref, cb_ref,
                wdn_ref, o_ref, u_s, act_s):
    tm = x_ref.shape[0]
    i = pl.program_id(1)
    x = x_ref[...]
    xe = jnp.concatenate([xp_ref[...], x, xnx_ref[...]], axis=0)
    xn = _modulate(xe, g_ref[...], sh_ref[0], sc_ref[0])
    row = lax.broadcasted_iota(jnp.int32, (tm + 2 * HALO, 1), 0)
    outside = ((row < HALO) & (i == 0)) | ((row >= tm + HALO) & (i == pl.num_programs(1) - 1))
    xn = jnp.where(outside, 0.0, xn).astype(BF16)
    cw = cw_ref[...]
    cb = cb_ref[...]
    for c in range(D_FF // FF_CHUNK):
        halves = []
        for half in range(2):
            sl = slice(half * D_FF + c * FF_CHUNK, half * D_FF + (c + 1) * FF_CHUNK)
            u_s[half] = jnp.dot(xn, wup_ref[:, sl], preferred_element_type=F32)
            acc = cb[:, sl]
            for k in range(FFN_CONV_W):
                acc = acc + cw[k:k + 1, sl] * u_s[half, pl.ds(HALO - FFN_CONV_LEFT + k, tm), :]
            halves.append(acc)
        a, g = halves
        act_s[:, c * FF_CHUNK:(c + 1) * FF_CHUNK] = ((a * jax.nn.sigmoid(a)) * g).astype(BF16)
    o_ref[...] = x + gate_ref[0] * jnp.dot(act_s[...], wdn_ref[...], preferred_element_type=F32)


def _ffn_call(x1, mods3, g_ffn, w_up, conv_w, conv_b, w_down):
    b, t, d = x1.shape
    tm = TOK_TILE
    hb = tm // HALO
    n_halo_blocks = t // HALO
    tok = pl.BlockSpec((None, tm, d), lambda bi, i: (bi, i, 0))
    prev = pl.BlockSpec((None, HALO, d), lambda bi, i: (bi, jnp.maximum(i * hb - 1, 0), 0))
    nxt = pl.BlockSpec((None, HALO, d), lambda bi, i: (bi, jnp.minimum((i + 1) * hb, n_halo_blocks - 1), 0))
    mod = lambda k: pl.BlockSpec((1, 1, D_MODEL), lambda bi, i: (bi * N_MOD + k, 0, 0))
    const = lambda shape: pl.BlockSpec(shape, lambda bi, i: (0,) * len(shape),
                                       pipeline_mode=pl.Buffered(1))
    bufs = [((tm, d), F32, 4), ((D_MODEL, 2 * D_FF), BF16, 1), ((D_FF, D_MODEL), BF16, 1),
            ((2, tm + 2 * HALO, FF_CHUNK), F32, 1), ((tm, D_FF), BF16, 1),
            ((tm + 2 * HALO, d), F32, 2)]
    return pl.pallas_call(
        _ffn_kernel,
        grid=(b, t // tm),
        in_specs=[tok, prev, nxt, mod(3), mod(4), mod(5), const((1, D_MODEL)),
                  const((D_MODEL, 2 * D_FF)), const((FFN_CONV_W, 2 * D_FF)), const((1, 2 * D_FF)),
                  const((D_FF, D_MODEL))],
        out_specs=tok,
        out_shape=jax.ShapeDtypeStruct((b, t, d), F32),
        scratch_shapes=[pltpu.VMEM((2, tm + 2 * HALO, FF_CHUNK), F32), pltpu.VMEM((tm, D_FF), BF16)],
        compiler_params=_params(("arbitrary", "arbitrary"), bufs),
        name="conv_ffn",
    )(x1, x1, x1, mods3, mods3, mods3, g_ffn, w_up, conv_w, conv_b, w_down)


def _rope_tables(t):
    n = HEAD_DIM // 4
    freq = ROPE_BASE ** (-jnp.arange(n, dtype=F32) / n)
    pos = jnp.arange(t)
    ang_r = (pos // GRID_W).astype(F32)[:, None] * freq
    ang_c = (pos % GRID_W).astype(F32)[:, None] * freq
    cos = jnp.concatenate([jnp.cos(ang_r)] * 2 + [jnp.cos(ang_c)] * 2, axis=-1)
    sin = jnp.concatenate([-jnp.sin(ang_r), jnp.sin(ang_r), -jnp.sin(ang_c), jnp.sin(ang_c)], axis=-1)
    reps = LANES // HEAD_DIM
    return jnp.tile(cos, (1, reps)), jnp.tile(sin, (1, reps))


def kernel(x, c, ctx, c_ctx, w_mod, b_mod, norm_mix_g, norm_ffn_g, w_in, lru_conv_w, lru_conv_b, lru_wa, lru_ba, lru_wx, lru_bx, lru_lambda, q_norm_g, k_norm_g, na_rpb, w_rnn_out, w_na_out, w_out, w_up, ffn_conv_w, ffn_conv_b, w_down):
    depth = w_mod.shape[0]
    assert depth == 1, "single-layer block"
    b, t, d = x.shape
    assert d == D_MODEL and b < MOD_ROWS and t % TOK_TILE == 0 and t % GRID_W == 0
    l = 0

    cc = jnp.zeros((MOD_ROWS, d), F32).at[:b].set(c).at[b].set(c_ctx)
    w_in_b = w_in[l].astype(BF16)
    g_mix = norm_mix_g[l][None, :]
    g_ffn = norm_ffn_g[l][None, :]
    q_gain = jnp.tile(q_norm_g[l], NA_HEADS)[None, :]
    k_gain = jnp.tile(k_norm_g[l], NA_HEADS)[None, :]
    head_of = jnp.arange(MXU_DIM) // HEAD_DIM
    e_mat = jnp.where(head_of[:, None] == head_of[None, :], 1.0 / HEAD_DIM, 0.0).astype(BF16)
    cos_t, sin_t = _rope_tables(t)
    wcat = jnp.concatenate([lru_wa[l], lru_wx[l]], axis=-1).astype(BF16)
    bcat = jnp.stack([lru_ba[l, 0], lru_bx[l, 0], lru_ba[l, 1], lru_bx[l, 1]])
    bias_tab = _bias_table(na_rpb[l], t // GRID_W)

    mods = _mod_call(cc, w_mod[l], b_mod[l][None, :])
    mods3 = mods.reshape(MOD_ROWS * N_MOD, 1, d)

    k_rot, v, q_rot, q_plain = _qkv_call(x, mods3, 0, g_mix, w_in_b, k_gain, q_gain, e_mat,
                                         cos_t, sin_t, latent=True)
    kc, vc = _qkv_call(ctx, mods3, b, g_mix, w_in_b, k_gain, None, e_mat, None, None, latent=False)
    y_rnn = _lru_call(x, ctx, mods3, g_mix, w_in_b, lru_conv_w[l], lru_conv_b[l][None, :],
                      wcat, bcat, lru_lambda[l])
    y_na = _attn_call(q_rot, q_plain, k_rot, v, kc, vc, bias_tab)
    x1 = _merge_call(x, y_rnn, y_na, mods3, g_mix, w_in_b, w_rnn_out[l].astype(BF16),
                     w_na_out[l].astype(BF16), w_out[l].astype(BF16))
    return _ffn_call(x1, mods3, g_ffn, w_up[l].astype(BF16), ffn_conv_w[l], ffn_conv_b[l][None, :],
                     w_down[l].astype(BF16))
```

```python
import functools

import jax
import jax.numpy as jnp
from jax import lax
from jax.experimental import pallas as pl
from jax.experimental.pallas import tpu as pltpu

F32 = jnp.float32
BF16 = jnp.bfloat16

D_MODEL = 1024
GRID_W = 64
D_RNN = D_MODEL
LRU_BLOCKS = 8
LRU_BLOCK_W = D_RNN // LRU_BLOCKS
LRU_CONV_W = 4
LRU_CONV_LEFT = 2
LRU_C = 8.0
NA_HEADS = 16
HEAD_DIM = 64
D_ATT = NA_HEADS * HEAD_DIM
NA_ROWS = 8
NA_COLS = 16
ROPE_BASE = 10000.0
D_FF = ((8 * D_MODEL // 3 + 127) // 128) * 128
FFN_CONV_W = 3
FFN_CONV_LEFT = 1
N_MOD = 6
EPS = 1e-6
NEG_INF = -1e30

COL_XR, COL_K, COL_V, COL_GX, COL_Q, COL_MR, COL_MN = range(7)

LANES = 128
SUBLANES = 8
MXU_DIM = 256
VMEM_LIMIT_CAP = 60000 * 1024
VMEM_TEMP_HEADROOM = 12 * 1024 * 1024

MOD_ROWS = 16
TOK_TILE = 512
LRU_CG = 256
LRU_CHUNK = 128
HALO = SUBLANES
FF_CHUNK = MXU_DIM
ATTN_GROUP = 4


def _nbytes(shape, dtype):
    n = 1
    for s in shape:
        n *= s
    return n * jnp.dtype(dtype).itemsize


def _vmem_limit(buffers):
    need = sum(_nbytes(s, d) * c for s, d, c in buffers) + VMEM_TEMP_HEADROOM
    return int(min(need, VMEM_LIMIT_CAP))


def _params(semantics, buffers):
    return pltpu.CompilerParams(dimension_semantics=semantics,
                                vmem_limit_bytes=_vmem_limit(buffers))


def _modulate(x, g, shift, scale):
    ms = jnp.mean(x * x, axis=-1, keepdims=True)
    y = x * lax.rsqrt(ms + EPS)
    return (y * g) * (1.0 + scale) + shift


def _mod_kernel(cc_ref, w_ref, b_ref, o_ref):
    s = cc_ref[...]
    s = s * jax.nn.sigmoid(s)
    o_ref[...] = jnp.dot(s.astype(BF16), w_ref[...].astype(BF16),
                         preferred_element_type=F32) + b_ref[...]


def _mod_call(cc, w_mod, b_mod):
    n = w_mod.shape[1]
    tn = D_MODEL
    bufs = [((MOD_ROWS, D_MODEL), F32, 2), ((D_MODEL, tn), F32, 2), ((MOD_ROWS, tn), F32, 2)]
    return pl.pallas_call(
        _mod_kernel,
        grid=(n // tn,),
        in_specs=[pl.BlockSpec((MOD_ROWS, D_MODEL), lambda j: (0, 0)),
                  pl.BlockSpec((D_MODEL, tn), lambda j: (0, j)),
                  pl.BlockSpec((1, tn), lambda j: (0, j))],
        out_specs=pl.BlockSpec((MOD_ROWS, tn), lambda j: (0, j)),
        out_shape=jax.ShapeDtypeStruct((MOD_ROWS, n), F32),
        compiler_params=_params(("arbitrary",), bufs),
        name="adaln_mod",
    )(cc, w_mod, b_mod)


def _head_rms(t, gain, e):
    sq = t * t
    hi = sq.astype(BF16)
    lo = (sq - hi.astype(F32)).astype(BF16)
    ms = jnp.dot(hi, e, preferred_element_type=F32) + jnp.dot(lo, e, preferred_element_type=F32)
    return (t * lax.rsqrt(ms + EPS)) * gain


def _rope(y, cos, sin_signed, first_half):
    partner = jnp.where(first_half, pltpu.roll(y, LANES - 16, 1), pltpu.roll(y, 16, 1))
    return y * cos + partner * sin_signed


def _qkv_kernel(*refs, latent):
    if latent:
        (x_ref, sh_ref, sc_ref, g_ref, wk_ref, wv_ref, wq_ref, kg_ref, qg_ref, e_ref,
         cos_ref, sin_ref, k_out, v_out, qr_out, qp_out) = refs
    else:
        (x_ref, sh_ref, sc_ref, g_ref, wk_ref, wv_ref, kg_ref, e_ref, k_out, v_out) = refs
    xn = _modulate(x_ref[...], g_ref[...], sh_ref[0], sc_ref[0]).astype(BF16)
    e = e_ref[...]
    if latent:
        cos = cos_ref[...]
        sin = sin_ref[...]
        lane = lax.broadcasted_iota(jnp.int32, cos.shape, 1)
        first_half = (lane % 32) < 16
    scale = HEAD_DIM ** -0.5
    for j in range(D_ATT // MXU_DIM):
        sl = slice(MXU_DIM * j, MXU_DIM * (j + 1))
        v_out[:, sl] = jnp.dot(xn, wv_ref[:, sl], preferred_element_type=F32).astype(BF16)
        k = _head_rms(jnp.dot(xn, wk_ref[:, sl], preferred_element_type=F32), kg_ref[:, sl], e)
        if not latent:
            k_out[:, sl] = k.astype(BF16)
            continue
        q = _head_rms(jnp.dot(xn, wq_ref[:, sl], preferred_element_type=F32), qg_ref[:, sl], e)
        qp_out[:, sl] = (q * scale).astype(BF16)
        for h in range(MXU_DIM // LANES):
            hl = slice(LANES * h, LANES * (h + 1))
            ol = slice(MXU_DIM * j + LANES * h, MXU_DIM * j + LANES * (h + 1))
            k_out[:, ol] = _rope(k[:, hl], cos, sin, first_half).astype(BF16)
            qr_out[:, ol] = (_rope(q[:, hl], cos, sin, first_half) * scale).astype(BF16)


def _qkv_call(x, mods3, mod_row0, g_mix, w_in, k_gain, q_gain, e_mat, cos_t, sin_t, latent):
    b, t, d = x.shape
    tm = min(TOK_TILE, t)
    nt = t // tm
    wblk = lambda c: pl.BlockSpec((D_MODEL, D_MODEL), lambda bi, i: (0, c))
    vec = pl.BlockSpec((1, D_MODEL), lambda bi, i: (0, 0))
    if latent:
        mod = lambda k: pl.BlockSpec((1, 1, D_MODEL), lambda bi, i: (bi * N_MOD + k, 0, 0))
    else:
        mod = lambda k: pl.BlockSpec((1, 1, D_MODEL), lambda bi, i: (mod_row0 * N_MOD + k, 0, 0))
    tok = pl.BlockSpec((None, tm, d), lambda bi, i: (bi, i, 0))
    in_specs = [tok, mod(0), mod(1), vec, wblk(COL_K), wblk(COL_V)]
    args = [x, mods3, mods3, g_mix, w_in, w_in]
    if latent:
        in_specs += [wblk(COL_Q), vec, vec]
        args += [w_in, k_gain, q_gain]
    else:
        in_specs += [vec]
        args += [k_gain]
    in_specs += [pl.BlockSpec((MXU_DIM, MXU_DIM), lambda bi, i: (0, 0))]
    args += [e_mat]
    n_out = 2
    if latent:
        rope_spec = pl.BlockSpec((tm, LANES), lambda bi, i: (i, 0))
        in_specs += [rope_spec, rope_spec]
        args += [cos_t, sin_t]
        n_out = 4
    out_sd = jax.ShapeDtypeStruct((b, t, D_ATT), BF16)
    bufs = [((tm, d), F32, 2), ((D_MODEL, D_MODEL), BF16, 6), ((tm, D_ATT), BF16, 2 * n_out),
            ((tm, LANES), F32, 4), ((tm, d), BF16, 1)]
    return pl.pallas_call(
        functools.partial(_qkv_kernel, latent=latent),
        grid=(b, nt),
        in_specs=in_specs,
        out_specs=[tok] * n_out,
        out_shape=[out_sd] * n_out,
        compiler_params=_params(("arbitrary", "arbitrary"), bufs),
        name="qkv_proj" if latent else "ctx_proj",
    )(*args)


def _softplus(x):
    return jnp.maximum(x, 0.0) + jnp.log1p(jnp.exp(-jnp.abs(x)))


def _lru_coeffs(src_ref, row0, n, conv_w, conv_b, wcat_ref, bcat, sp, d):
    xe = src_ref[pl.ds(row0, n + 2 * HALO), :]
    xc = conv_b
    for k in range(LRU_CONV_W):
        off = HALO - LRU_CONV_LEFT + k
        xc = xc + conv_w[k:k + 1, :] * xe[off:off + n]
    xcb = xc.astype(BF16)
    rs, gs = [], []
    for blk in range(LRU_CG // LRU_BLOCK_W):
        bl = slice(LRU_BLOCK_W * blk, LRU_BLOCK_W * (blk + 1))
        z = jnp.dot(xcb[:, bl], wcat_ref[d, blk], preferred_element_type=F32)
        rs.append(z[:, :LRU_BLOCK_W])
        gs.append(z[:, LRU_BLOCK_W:])
    r = jax.nn.sigmoid(jnp.concatenate(rs, axis=-1) + bcat[2 * d:2 * d + 1, :])
    gate = jax.nn.sigmoid(jnp.concatenate(gs, axis=-1) + bcat[2 * d + 1:2 * d + 2, :])
    log_a = (-LRU_C * r) * sp[d:d + 1, :]
    a = jnp.exp(log_a)
    b = jnp.sqrt(-jnp.tanh(log_a) * (a * a + 1.0)) * gate * xc
    return a, b


def _scan_chunk(a, b, carry, reverse):
    n, c = a.shape
    ng = n // SUBLANES
    av = a.reshape(ng, SUBLANES, c)
    bv = b.reshape(ng, SUBLANES, c)
    rid = lax.broadcasted_iota(jnp.int32, (ng, SUBLANES, c), 1)
    for s in (1, 2, 4):
        shift = SUBLANES - s if reverse else s
        m = (rid < SUBLANES - s) if reverse else (rid >= s)
        a_nb = pltpu.roll(av, shift, 1)
        b_nb = pltpu.roll(bv, shift, 1)
        bv = jnp.where(m, av * b_nb + bv, bv)
        av = jnp.where(m, av * a_nb, av)
    edge = 0 if reverse else SUBLANES - 1
    order = range(ng - 1, -1, -1) if reverse else range(ng)
    hs = [None] * ng
    for g in order:
        hs[g] = bv[g] + av[g] * carry
        carry = av[g, edge:edge + 1, :] * carry + bv[g, edge:edge + 1, :]
    return jnp.concatenate(hs, axis=0), carry


def _lru_kernel(x_ref, ctx_ref, shx_ref, scx_ref, shc_ref, scc_ref, g_ref, wxr_ref, wgx_ref,
                cw_ref, cb_ref, wcat_ref, bcat_ref, lam_ref, y_ref,
                xn_s, cn_s, xr_s, cr_s, gg_s, hf_s):
    t = x_ref.shape[0]
    nctx = ctx_ref.shape[0]
    cg = y_ref.shape[1]

    @pl.when(pl.program_id(1) == 0)
    def _():
        def body(i, _):
            r0 = pl.multiple_of(i * TOK_TILE, TOK_TILE)
            xn_s[pl.ds(r0, TOK_TILE), :] = _modulate(
                x_ref[pl.ds(r0, TOK_TILE), :], g_ref[...], shx_ref[0], scx_ref[0]).astype(BF16)
            return 0
        lax.fori_loop(0, t // TOK_TILE, body, 0)
        cn_s[...] = _modulate(ctx_ref[...], g_ref[...], shc_ref[0], scc_ref[0]).astype(BF16)

    zero_halo = jnp.zeros((HALO, cg), F32)
    xr_s[0:HALO, :] = zero_halo
    xr_s[t + HALO:t + 2 * HALO, :] = zero_halo
    cr_s[0:HALO, :] = zero_halo
    cr_s[nctx + HALO:nctx + 2 * HALO, :] = zero_halo

    def proj(i, _):
        r0 = pl.multiple_of(i * TOK_TILE, TOK_TILE)
        xn = xn_s[pl.ds(r0, TOK_TILE), :]
        xr_s[pl.ds(r0 + HALO, TOK_TILE), :] = jnp.dot(xn, wxr_ref[...], preferred_element_type=F32)
        gg_s[pl.ds(r0, TOK_TILE), :] = jax.nn.gelu(
            jnp.dot(xn, wgx_ref[...], preferred_element_type=F32))
        return 0
    lax.fori_loop(0, t // TOK_TILE, proj, 0)
    cr_s[HALO:HALO + nctx, :] = jnp.dot(cn_s[...], wxr_ref[...], preferred_element_type=F32)

    conv_w = cw_ref[...]
    conv_b = cb_ref[...]
    bcat = bcat_ref[...]
    sp = _softplus(-lam_ref[...])
    coeffs = functools.partial(_lru_coeffs, n=LRU_CHUNK, conv_w=conv_w, conv_b=conv_b,
                               wcat_ref=wcat_ref, bcat=bcat, sp=sp)
    n_lat = t // LRU_CHUNK
    n_ctx = nctx // LRU_CHUNK

    carry = jnp.zeros((1, cg), F32)
    for i in range(n_ctx):
        a, b = coeffs(cr_s, i * LRU_CHUNK, d=0)
        _, carry = _scan_chunk(a, b, carry, reverse=False)

    def fwd(i, carry):
        r0 = pl.multiple_of(i * LRU_CHUNK, LRU_CHUNK)
        a, b = coeffs(xr_s, r0, d=0)
        h, carry = _scan_chunk(a, b, carry, reverse=False)
        hf_s[pl.ds(r0, LRU_CHUNK), :] = h
        return carry
    lax.fori_loop(0, n_lat, fwd, carry)

    carry = jnp.zeros((1, cg), F32)
    for i in range(n_ctx - 1, -1, -1):
        a, b = coeffs(cr_s, i * LRU_CHUNK, d=1)
        _, carry = _scan_chunk(a, b, carry, reverse=True)

    def bwd(j, carry):
        r0 = pl.multiple_of((n_lat - 1 - j) * LRU_CHUNK, LRU_CHUNK)
        a, b = coeffs(xr_s, r0, d=1)
        h, carry = _scan_chunk(a, b, carry, reverse=True)
        rows = pl.ds(r0, LRU_CHUNK)
        y_ref[rows, :] = ((hf_s[rows, :] + h) * gg_s[rows, :]).astype(BF16)
        return carry
    lax.fori_loop(0, n_lat, bwd, carry)


def _lru_call(x, ctx, mods3, g_mix, w_in, conv_w, conv_b, wcat, bcat, lam):
    b, t, d = x.shape
    nctx = ctx.shape[1]
    cg = LRU_CG
    ng = D_RNN // cg
    ctx_row = b * N_MOD
    mod = lambda k: pl.BlockSpec((1, 1, D_MODEL), lambda bi, g: (bi * N_MOD + k, 0, 0))
    modc = lambda k: pl.BlockSpec((1, 1, D_MODEL), lambda bi, g: (ctx_row + k, 0, 0))
    in_specs = [
        pl.BlockSpec((None, t, d), lambda bi, g: (bi, 0, 0)),
        pl.BlockSpec((None, nctx, d), lambda bi, g: (bi, 0, 0)),
        mod(0), mod(1), modc(0), modc(1),
        pl.BlockSpec((1, D_MODEL), lambda bi, g: (0, 0)),
        pl.BlockSpec((D_MODEL, cg), lambda bi, g: (0, COL_XR * ng + g)),
        pl.BlockSpec((D_MODEL, cg), lambda bi, g: (0, COL_GX * ng + g)),
        pl.BlockSpec((LRU_CONV_W, cg), lambda bi, g: (0, g)),
        pl.BlockSpec((1, cg), lambda bi, g: (0, g)),
        pl.BlockSpec((2, cg // LRU_BLOCK_W, LRU_BLOCK_W, 2 * LRU_BLOCK_W), lambda bi, g: (0, g, 0, 0)),
        pl.BlockSpec((4, cg), lambda bi, g: (0, g)),
        pl.BlockSpec((2, cg), lambda bi, g: (0, g)),
    ]
    scratch = [
        pltpu.VMEM((t, d), BF16), pltpu.VMEM((nctx, d), BF16),
        pltpu.VMEM((t + 2 * HALO, cg), F32), pltpu.VMEM((nctx + 2 * HALO, cg), F32),
        pltpu.VMEM((t, cg), F32), pltpu.VMEM((t, cg), F32),
    ]
    bufs = [((t, d), F32, 2), ((nctx, d), F32, 2), ((D_MODEL, cg), BF16, 4), ((t, cg), BF16, 2),
            ((t, d), BF16, 1), ((nctx, d), BF16, 1), ((t + 2 * HALO, cg), F32, 3),
            ((nctx + 2 * HALO, cg), F32, 1)]
    return pl.pallas_call(
        _lru_kernel,
        grid=(b, ng),
        in_specs=in_specs,
        out_specs=pl.BlockSpec((None, t, cg), lambda bi, g: (bi, 0, g)),
        out_shape=jax.ShapeDtypeStruct((b, t, D_RNN), BF16),
        scratch_shapes=scratch,
        compiler_params=_params(("arbitrary", "arbitrary"), bufs),
        name="rglru",
    )(x, ctx, mods3, mods3, mods3, mods3, g_mix, w_in, w_in, conv_w, conv_b, wcat, bcat, lam)


NT_DIMS = (((1,), (1,)), ((), ()))
KEY_ROWS_PER_TILE = LANES // GRID_W


def _attn_group_geometry(g, rows):
    r0 = g * ATTN_GROUP
    starts = [min(max(r0 + o - NA_ROWS // 2, 0), rows - NA_ROWS) for o in range(ATTN_GROUP)]
    key_row0 = starts[0]
    n_key_rows = -(-(starts[-1] + NA_ROWS - key_row0) // KEY_ROWS_PER_TILE) * KEY_ROWS_PER_TILE
    desc = tuple((starts[o] - key_row0, key_row0 - (r0 + o)) for o in range(ATTN_GROUP))
    return key_row0, n_key_rows, desc


def _attn_variants(rows):
    variants = []
    for g in range(rows // ATTN_GROUP):
        v = _attn_group_geometry(g, rows)[1:]
        if v not in variants:
            variants.append(v)
    return variants


def _attn_group_plan(g, rows):
    key_row0, n_key_rows, desc = _attn_group_geometry(g, rows)
    variant = _attn_variants(rows).index((n_key_rows, desc))
    tiles = [(off // KEY_ROWS_PER_TILE, -(-(off + NA_ROWS) // KEY_ROWS_PER_TILE)) for off, _ in desc]
    return variant, key_row0, n_key_rows, tiles


def _attn_kernel(qr_ref, qp_ref, k_ref, v_ref, kc_ref, vc_ref, bias_ref, y_ref, qs_s, qps_s):
    t = k_ref.shape[0]
    rows = t // GRID_W
    lane = lax.broadcasted_iota(jnp.int32, (rows, GRID_W, LANES), 2)
    head_a = lane < HEAD_DIM

    def stack_heads(q_ref, dst):
        q3 = q_ref[...].reshape(rows, GRID_W, LANES)
        zero = jnp.zeros_like(q3)
        dst[...] = jnp.concatenate([jnp.where(head_a, q3, zero), jnp.where(head_a, zero, q3)], axis=1)
    stack_heads(qr_ref, qs_s)
    stack_heads(qp_ref, qps_s)

    kc = kc_ref[...]
    vc = vc_ref[...]
    out_a = lax.broadcasted_iota(jnp.int32, (GRID_W, LANES), 1) < HEAD_DIM
    rows_stacked = 2 * GRID_W

    def lane_tiles(arrays):
        return [a[:, LANES * i:LANES * (i + 1)] for a in arrays for i in range(a.shape[1] // LANES)]

    for g in range(rows // ATTN_GROUP):
        r0 = g * ATTN_GROUP
        variant, key_row0, n_key_rows, tiles = _attn_group_plan(g, rows)
        k_sup = k_ref[key_row0 * GRID_W:(key_row0 + n_key_rows) * GRID_W, :]
        v_sup = v_ref[key_row0 * GRID_W:(key_row0 + n_key_rows) * GRID_W, :]
        n_tiles = n_key_rows // KEY_ROWS_PER_TILE
        q_grp = qs_s[r0:r0 + ATTN_GROUP].reshape(ATTN_GROUP * rows_stacked, LANES)
        qp_grp = qps_s[r0:r0 + ATTN_GROUP].reshape(ATTN_GROUP * rows_stacked, LANES)
        s_lat = lax.dot_general(q_grp, k_sup, NT_DIMS, preferred_element_type=F32)
        s_ctx = lax.dot_general(qp_grp, kc, NT_DIMS, preferred_element_type=F32)
        p_rows, pc_rows, denoms = [], [], []
        for o in range(ATTN_GROUP):
            lo, hi = tiles[o]
            rsl = slice(o * rows_stacked, (o + 1) * rows_stacked)
            csl = slice(lo * LANES, hi * LANES)
            s_o = s_lat[rsl, csl] + bias_ref[variant, rsl, csl]
            sc_o = s_ctx[rsl]
            m = jnp.max(functools.reduce(jnp.maximum, lane_tiles([s_o, sc_o])), axis=-1, keepdims=True)
            p_o = jnp.exp(s_o - m)
            pc_o = jnp.exp(sc_o - m)
            denoms.append(jnp.sum(functools.reduce(jnp.add, lane_tiles([p_o, pc_o])), axis=-1, keepdims=True))
            pieces = [p_o.astype(BF16)]
            if lo > 0:
                pieces.insert(0, jnp.zeros((rows_stacked, lo * LANES), BF16))
            if hi < n_tiles:
                pieces.append(jnp.zeros((rows_stacked, (n_tiles - hi) * LANES), BF16))
            p_rows.append(jnp.concatenate(pieces, axis=-1) if len(pieces) > 1 else pieces[0])
            pc_rows.append(pc_o.astype(BF16))
        o_grp = (jnp.dot(jnp.concatenate(p_rows, axis=0), v_sup, preferred_element_type=F32)
                 + jnp.dot(jnp.concatenate(pc_rows, axis=0), vc, preferred_element_type=F32))
        o_grp = o_grp / jnp.concatenate(denoms, axis=0)
        for o in range(ATTN_GROUP):
            blk = o_grp[o * rows_stacked:(o + 1) * rows_stacked]
            y_ref[(r0 + o) * GRID_W:(r0 + o + 1) * GRID_W, :] = jnp.where(
                out_a, blk[:GRID_W], blk[GRID_W:]).astype(BF16)


def _attn_call(q_rot, q_plain, k_rot, v, kc, vc, bias_tab):
    b, t, _ = q_rot.shape
    nctx = kc.shape[1]
    rows = t // GRID_W
    npair = D_ATT // LANES
    tok = pl.BlockSpec((None, t, LANES), lambda bi, hp: (bi, 0, hp))
    ctx = pl.BlockSpec((None, nctx, LANES), lambda bi, hp: (bi, 0, hp))
    bias_shape = bias_tab.shape[1:]
    bufs = [((t, LANES), BF16, 10), ((nctx, LANES), BF16, 4), (bias_shape, F32, 2),
            ((rows, 2 * GRID_W, LANES), BF16, 2)]
    return pl.pallas_call(
        _attn_kernel,
        grid=(b, npair),
        in_specs=[tok, tok, tok, tok, ctx, ctx,
                  pl.BlockSpec((None,) + bias_shape, lambda bi, hp: (hp, 0, 0, 0))],
        out_specs=tok,
        out_shape=jax.ShapeDtypeStruct((b, t, D_ATT), BF16),
        scratch_shapes=[pltpu.VMEM((rows, 2 * GRID_W, LANES), BF16),
                        pltpu.VMEM((rows, 2 * GRID_W, LANES), BF16)],
        compiler_params=_params(("arbitrary", "arbitrary"), bufs),
        name="nbr_attn",
    )(q_rot, q_plain, k_rot, v, kc, vc, bias_tab)


def _bias_table(rpb, rows):
    c = jnp.arange(GRID_W)
    col_start = jnp.clip(c - NA_COLS // 2, 0, GRID_W - NA_COLS)
    in_win = (c[None, :] >= col_start[:, None]) & (c[None, :] < col_start[:, None] + NA_COLS)
    dc = jnp.clip(c[None, :] - c[:, None], -(NA_COLS - 1), NA_COLS - 1) + (NA_COLS - 1)
    toe = jnp.where(in_win[None, None], rpb[:, :, dc], NEG_INF)
    variants = _attn_variants(rows)
    max_key_rows = max(n for n, _ in variants)
    j = jnp.arange(max_key_rows)
    blocks = []
    for _, desc in variants:
        per_row = []
        for off, base in desc:
            dr = jnp.clip(base + j + (NA_ROWS - 1), 0, 2 * NA_ROWS - 2)
            valid = (j >= off) & (j < off + NA_ROWS)
            blk = jnp.where(valid[None, :, None, None], toe[:, dr], NEG_INF)
            per_row.append(blk.transpose(0, 2, 1, 3).reshape(NA_HEADS, GRID_W, max_key_rows * GRID_W))
        v = jnp.stack(per_row, axis=1)
        v = v.reshape(NA_HEADS // 2, 2, ATTN_GROUP, GRID_W, max_key_rows * GRID_W)
        blocks.append(v.transpose(0, 2, 1, 3, 4).reshape(
            NA_HEADS // 2, ATTN_GROUP * 2 * GRID_W, max_key_rows * GRID_W))
    return jnp.stack(blocks, axis=1)


def _merge_kernel(x_ref, yr_ref, yn_ref, sh_ref, sc_ref, gate_ref, g_ref,
                  wgr_ref, wgn_ref, wr_ref, wn_ref, wo_ref, o_ref):
    x = x_ref[...]
    xn = _modulate(x, g_ref[...], sh_ref[0], sc_ref[0]).astype(BF16)
    z_r = jnp.dot(yr_ref[...], wr_ref[...], preferred_element_type=F32)
    z_n = jnp.dot(yn_ref[...], wn_ref[...], preferred_element_type=F32)
    g_r = jax.nn.sigmoid(jnp.dot(xn, wgr_ref[...], preferred_element_type=F32))
    g_n = jax.nn.sigmoid(jnp.dot(xn, wgn_ref[...], preferred_element_type=F32))
    merged = g_r * z_r + g_n * z_n
    o_ref[...] = x + gate_ref[0] * jnp.dot(merged.astype(BF16), wo_ref[...], preferred_element_type=F32)


def _merge_call(x, y_rnn, y_na, mods3, g_mix, w_in, w_rnn_out, w_na_out, w_out):
    b, t, d = x.shape
    tm = TOK_TILE
    tok = pl.BlockSpec((None, tm, d), lambda bi, i: (bi, i, 0))
    mod = lambda k: pl.BlockSpec((1, 1, D_MODEL), lambda bi, i: (bi * N_MOD + k, 0, 0))
    wblk = lambda c: pl.BlockSpec((D_MODEL, D_MODEL), lambda bi, i: (0, c))
    bufs = [((tm, d), F32, 4), ((tm, d), BF16, 4), ((D_MODEL, D_MODEL), BF16, 10), ((tm, d), F32, 4)]
    return pl.pallas_call(
        _merge_kernel,
        grid=(b, t // tm),
        in_specs=[tok, tok, tok, mod(0), mod(1), mod(2),
                  pl.BlockSpec((1, D_MODEL), lambda bi, i: (0, 0)),
                  wblk(COL_MR), wblk(COL_MN), wblk(0), wblk(0), wblk(0)],
        out_specs=tok,
        out_shape=jax.ShapeDtypeStruct((b, t, d), F32),
        compiler_params=_params(("arbitrary", "arbitrary"), bufs),
        name="merge",
    )(x, y_rnn, y_na, mods3, mods3, mods3, g_mix, w_in, w_in, w_rnn_out, w_na_out, w_out)


def _ffn_kernel(x_ref, xp_ref, xnx_ref, sh_ref, sc_ref, gate_ref, g_ref, wup_ref, cw_ref, cb_ref,
                wdn_ref, o_ref, u_s, act_s):
    tm = x_ref.shape[0]
    i = pl.program_id(1)
    x = x_ref[...]
    xe = jnp.concatenate([xp_ref[...], x, xnx_ref[...]], axis=0)
    xn = _modulate(xe, g_ref[...], sh_ref[0], sc_ref[0])
    row = lax.broadcasted_iota(jnp.int32, (tm + 2 * HALO, 1), 0)
    outside = ((row < HALO) & (i == 0)) | ((row >= tm + HALO) & (i == pl.num_programs(1) - 1))
    xn = jnp.where(outside, 0.0, xn).astype(BF16)
    cw = cw_ref[...]
    cb = cb_ref[...]
    for c in range(D_FF // FF_CHUNK):
        halves = []
        for half in range(2):
            sl = slice(half * D_FF + c * FF_CHUNK, half * D_FF + (c + 1) * FF_CHUNK)
            u_s[half] = jnp.dot(xn, wup_ref[:, sl], preferred_element_type=F32)
            acc = cb[:, sl]
            for k in range(FFN_CONV_W):
                acc = acc + cw[k:k + 1, sl] * u_s[half, pl.ds(HALO - FFN_CONV_LEFT + k, tm), :]
            halves.append(acc)
        a, g = halves
        act_s[:, c * FF_CHUNK:(c + 1) * FF_CHUNK] = ((a * jax.nn.sigmoid(a)) * g).astype(BF16)
    o_ref[...] = x + gate_ref[0] * jnp.dot(act_s[...], wdn_ref[...], preferred_element_type=F32)


def _ffn_call(x1, mods3, g_ffn, w_up, conv_w, conv_b, w_down):
    b, t, d = x1.shape
    tm = TOK_TILE
    hb = tm // HALO
    n_halo_blocks = t // HALO
    tok = pl.BlockSpec((None, tm, d), lambda bi, i: (bi, i, 0))
    prev = pl.BlockSpec((None, HALO, d), lambda bi, i: (bi, jnp.maximum(i * hb - 1, 0), 0))
    nxt = pl.BlockSpec((None, HALO, d), lambda bi, i: (bi, jnp.minimum((i + 1) * hb, n_halo_blocks - 1), 0))
    mod = lambda k: pl.BlockSpec((1, 1, D_MODEL), lambda bi, i: (bi * N_MOD + k, 0, 0))
    const = lambda shape: pl.BlockSpec(shape, lambda bi, i: (0,) * len(shape),
                                       pipeline_mode=pl.Buffered(1))
    bufs = [((tm, d), F32, 4), ((D_MODEL, 2 * D_FF), BF16, 1), ((D_FF, D_MODEL), BF16, 1),
            ((2, tm + 2 * HALO, FF_CHUNK), F32, 1), ((tm, D_FF), BF16, 1),
            ((tm + 2 * HALO, d), F32, 2)]
    return pl.pallas_call(
        _ffn_kernel,
        grid=(b, t // tm),
        in_specs=[tok, prev, nxt, mod(3), mod(4), mod(5), const((1, D_MODEL)),
                  const((D_MODEL, 2 * D_FF)), const((FFN_CONV_W, 2 * D_FF)), const((1, 2 * D_FF)),
                  const((D_FF, D_MODEL))],
        out_specs=tok,
        out_shape=jax.ShapeDtypeStruct((b, t, d), F32),
        scratch_shapes=[pltpu.VMEM((2, tm + 2 * HALO, FF_CHUNK), F32), pltpu.VMEM((tm, D_FF), BF16)],
        compiler_params=_params(("arbitrary", "arbitrary"), bufs),
        name="conv_ffn",
    )(x1, x1, x1, mods3, mods3, mods3, g_ffn, w_up, conv_w, conv_b, w_down)


def _rope_tables(t):
    n = HEAD_DIM // 4
    freq = ROPE_BASE ** (-jnp.arange(n, dtype=F32) / n)
    pos = jnp.arange(t)
    ang_r = (pos // GRID_W).astype(F32)[:, None] * freq
    ang_c = (pos % GRID_W).astype(F32)[:, None] * freq
    cos = jnp.concatenate([jnp.cos(ang_r)] * 2 + [jnp.cos(ang_c)] * 2, axis=-1)
    sin = jnp.concatenate([-jnp.sin(ang_r), jnp.sin(ang_r), -jnp.sin(ang_c), jnp.sin(ang_c)], axis=-1)
    reps = LANES // HEAD_DIM
    return jnp.tile(cos, (1, reps)), jnp.tile(sin, (1, reps))


def kernel(x, c, ctx, c_ctx, w_mod, b_mod, norm_mix_g, norm_ffn_g, w_in, lru_conv_w, lru_conv_b, lru_wa, lru_ba, lru_wx, lru_bx, lru_lambda, q_norm_g, k_norm_g, na_rpb, w_rnn_out, w_na_out, w_out, w_up, ffn_conv_w, ffn_conv_b, w_down):
    depth = w_mod.shape[0]
    assert depth == 1, "single-layer block"
    b, t, d = x.shape
    assert d == D_MODEL and b < MOD_ROWS and t % TOK_TILE == 0 and t % GRID_W == 0
    l = 0

    cc = jnp.zeros((MOD_ROWS, d), F32).at[:b].set(c).at[b].set(c_ctx)
    w_in_b = w_in[l].astype(BF16)
    g_mix = norm_mix_g[l][None, :]
    g_ffn = norm_ffn_g[l][None, :]
    q_gain = jnp.tile(q_norm_g[l], NA_HEADS)[None, :]
    k_gain = jnp.tile(k_norm_g[l], NA_HEADS)[None, :]
    head_of = jnp.arange(MXU_DIM) // HEAD_DIM
    e_mat = jnp.where(head_of[:, None] == head_of[None, :], 1.0 / HEAD_DIM, 0.0).astype(BF16)
    cos_t, sin_t = _rope_tables(t)
    wcat = jnp.concatenate([lru_wa[l], lru_wx[l]], axis=-1).astype(BF16)
    bcat = jnp.stack([lru_ba[l, 0], lru_bx[l, 0], lru_ba[l, 1], lru_bx[l, 1]])
    bias_tab = _bias_table(na_rpb[l], t // GRID_W)

    mods = _mod_call(cc, w_mod[l], b_mod[l][None, :])
    mods3 = mods.reshape(MOD_ROWS * N_MOD, 1, d)

    k_rot, v, q_rot, q_plain = _qkv_call(x, mods3, 0, g_mix, w_in_b, k_gain, q_gain, e_mat,
                                         cos_t, sin_t, latent=True)
    kc, vc = _qkv_call(ctx, mods3, b, g_mix, w_in_b, k_gain, None, e_mat, None, None, latent=False)
    y_rnn = _lru_call(x, ctx, mods3, g_mix, w_in_b, lru_conv_w[l], lru_conv_b[l][None, :],
                      wcat, bcat, lru_lambda[l])
    y_na = _attn_call(q_rot, q_plain, k_rot, v, kc, vc, bias_tab)
    x1 = _merge_call(x, y_rnn, y_na, mods3, g_mix, w_in_b, w_rnn_out[l].astype(BF16),
                     w_na_out[l].astype(BF16), w_out[l].astype(BF16))
    return _ffn_call(x1, mods3, g_ffn, w_up[l].astype(BF16), ffn_conv_w[l], ffn_conv_b[l][None, :],
                     w_down[l].astype(BF16))
```

```python
import functools

import jax
import jax.numpy as jnp
import numpy as np
from jax import lax
from jax.experimental import pallas as pl
from jax.experimental.pallas import tpu as pltpu

F32 = jnp.float32
BF16 = jnp.bfloat16

D_MODEL = 1024
GRID_W = 64
D_RNN = D_MODEL
LRU_BLOCKS = 8
LRU_BLOCK_W = D_RNN // LRU_BLOCKS
LRU_CONV_W = 4
LRU_CONV_LEFT = 2
LRU_C = 8.0
NA_HEADS = 16
HEAD_DIM = 64
D_ATT = NA_HEADS * HEAD_DIM
NA_ROWS = 8
NA_COLS = 16
ROPE_BASE = 10000.0
D_FF = ((8 * D_MODEL // 3 + 127) // 128) * 128
FFN_CONV_W = 3
FFN_CONV_LEFT = 1
N_MOD = 6
EPS = 1e-6
NEG_INF = -1e30

COL_XR, COL_K, COL_V, COL_GX, COL_Q, COL_MR, COL_MN = range(7)

LANES = 128
SUBLANES = 8
MXU_DIM = 256
VMEM_LIMIT_CAP = 60000 * 1024
VMEM_TEMP_HEADROOM = 12 * 1024 * 1024

MOD_ROWS = 16
TOK_TILE = 512
LRU_CG = 256
LRU_CHUNK = 128
LRU_UNROLL = 2
HALO = SUBLANES
FF_CHUNK = MXU_DIM
ATTN_GROUP = 4


def _nbytes(shape, dtype):
    n = 1
    for s in shape:
        n *= s
    return n * jnp.dtype(dtype).itemsize


def _vmem_limit(buffers):
    need = sum(_nbytes(s, d) * c for s, d, c in buffers) + VMEM_TEMP_HEADROOM
    return int(min(need, VMEM_LIMIT_CAP))


def _params(semantics, buffers):
    return pltpu.CompilerParams(dimension_semantics=semantics,
                                vmem_limit_bytes=_vmem_limit(buffers))


def _modulate(x, g, shift, scale):
    ms = jnp.mean(x * x, axis=-1, keepdims=True)
    y = x * lax.rsqrt(ms + EPS)
    return (y * g) * (1.0 + scale) + shift


def _mod_kernel(cc_ref, w_ref, b_ref, o_ref):
    s = cc_ref[...]
    s = s * jax.nn.sigmoid(s)
    o_ref[...] = jnp.dot(s.astype(BF16), w_ref[...].astype(BF16),
                         preferred_element_type=F32) + b_ref[...]


def _mod_call(cc, w_mod, b_mod):
    n = w_mod.shape[1]
    tn = D_MODEL
    bufs = [((MOD_ROWS, D_MODEL), F32, 2), ((D_MODEL, tn), F32, 2), ((MOD_ROWS, tn), F32, 2)]
    return pl.pallas_call(
        _mod_kernel,
        grid=(n // tn,),
        in_specs=[pl.BlockSpec((MOD_ROWS, D_MODEL), lambda j: (0, 0)),
                  pl.BlockSpec((D_MODEL, tn), lambda j: (0, j)),
                  pl.BlockSpec((1, tn), lambda j: (0, j))],
        out_specs=pl.BlockSpec((MOD_ROWS, tn), lambda j: (0, j)),
        out_shape=jax.ShapeDtypeStruct((MOD_ROWS, n), F32),
        compiler_params=_params(("arbitrary",), bufs),
        name="adaln_mod",
    )(cc, w_mod, b_mod)


def _head_rms(t, gain, e):
    sq = t * t
    hi = sq.astype(BF16)
    lo = (sq - hi.astype(F32)).astype(BF16)
    ms = jnp.dot(hi, e, preferred_element_type=F32) + jnp.dot(lo, e, preferred_element_type=F32)
    return (t * lax.rsqrt(ms + EPS)) * gain


def _rope(y, cos, sin_signed, first_half):
    partner = jnp.where(first_half, pltpu.roll(y, LANES - 16, 1), pltpu.roll(y, 16, 1))
    return y * cos + partner * sin_signed


def _qkv_kernel(*refs, latent):
    if latent:
        (x_ref, sh_ref, sc_ref, g_ref, wk_ref, wv_ref, wq_ref, kg_ref, qg_ref, e_ref,
         cos_ref, sin_ref, k_out, v_out, qr_out, qp_out) = refs
    else:
        (x_ref, sh_ref, sc_ref, g_ref, wk_ref, wv_ref, kg_ref, e_ref, k_out, v_out) = refs
    xn = _modulate(x_ref[...], g_ref[...], sh_ref[0], sc_ref[0]).astype(BF16)
    e = e_ref[...]
    if latent:
        cos = cos_ref[...]
        sin = sin_ref[...]
        lane = lax.broadcasted_iota(jnp.int32, cos.shape, 1)
        first_half = (lane % 32) < 16
    scale = HEAD_DIM ** -0.5
    for j in range(D_ATT // MXU_DIM):
        sl = slice(MXU_DIM * j, MXU_DIM * (j + 1))
        v_out[:, sl] = jnp.dot(xn, wv_ref[:, sl], preferred_element_type=F32).astype(BF16)
        k = _head_rms(jnp.dot(xn, wk_ref[:, sl], preferred_element_type=F32), kg_ref[:, sl], e)
        if not latent:
            k_out[:, sl] = k.astype(BF16)
            continue
        q = _head_rms(jnp.dot(xn, wq_ref[:, sl], preferred_element_type=F32), qg_ref[:, sl], e)
        qp_out[:, sl] = (q * scale).astype(BF16)
        for h in range(MXU_DIM // LANES):
            hl = slice(LANES * h, LANES * (h + 1))
            ol = slice(MXU_DIM * j + LANES * h, MXU_DIM * j + LANES * (h + 1))
            k_out[:, ol] = _rope(k[:, hl], cos, sin, first_half).astype(BF16)
            qr_out[:, ol] = (_rope(q[:, hl], cos, sin, first_half) * scale).astype(BF16)


def _qkv_call(x, mods3, mod_row0, g_mix, w_in, k_gain, q_gain, e_mat, cos_t, sin_t, latent):
    b, t, d = x.shape
    tm = min(TOK_TILE, t)
    nt = t // tm
    wblk = lambda c: pl.BlockSpec((D_MODEL, D_MODEL), lambda bi, i: (0, c))
    vec = pl.BlockSpec((1, D_MODEL), lambda bi, i: (0, 0))
    if latent:
        mod = lambda k: pl.BlockSpec((1, 1, D_MODEL), lambda bi, i: (bi * N_MOD + k, 0, 0))
    else:
        mod = lambda k: pl.BlockSpec((1, 1, D_MODEL), lambda bi, i: (mod_row0 * N_MOD + k, 0, 0))
    tok = pl.BlockSpec((None, tm, d), lambda bi, i: (bi, i, 0))
    in_specs = [tok, mod(0), mod(1), vec, wblk(COL_K), wblk(COL_V)]
    args = [x, mods3, mods3, g_mix, w_in, w_in]
    if latent:
        in_specs += [wblk(COL_Q), vec, vec]
        args += [w_in, k_gain, q_gain]
    else:
        in_specs += [vec]
        args += [k_gain]
    in_specs += [pl.BlockSpec((MXU_DIM, MXU_DIM), lambda bi, i: (0, 0))]
    args += [e_mat]
    n_out = 2
    if latent:
        rope_spec = pl.BlockSpec((tm, LANES), lambda bi, i: (i, 0))
        in_specs += [rope_spec, rope_spec]
        args += [cos_t, sin_t]
        n_out = 4
    out_sd = jax.ShapeDtypeStruct((b, t, D_ATT), BF16)
    bufs = [((tm, d), F32, 2), ((D_MODEL, D_MODEL), BF16, 6), ((tm, D_ATT), BF16, 2 * n_out),
            ((tm, LANES), F32, 4), ((tm, d), BF16, 1)]
    return pl.pallas_call(
        functools.partial(_qkv_kernel, latent=latent),
        grid=(b, nt),
        in_specs=in_specs,
        out_specs=[tok] * n_out,
        out_shape=[out_sd] * n_out,
        compiler_params=_params(("arbitrary", "arbitrary"), bufs),
        name="qkv_proj" if latent else "ctx_proj",
    )(*args)


def _softplus(x):
    return jnp.maximum(x, 0.0) + jnp.log1p(jnp.exp(-jnp.abs(x)))


SEG_BLOCK = SUBLANES * SUBLANES
LRU_HALO_FRONT = 2 * SUBLANES
LRU_HALO_BACK = SUBLANES


def _seg_permutation(n):
    p = np.arange(n)
    src = (p // SEG_BLOCK) * SEG_BLOCK + (p % SUBLANES) * SUBLANES + (p % SEG_BLOCK) // SUBLANES
    return jnp.asarray(src[:, None] == p[None, :], BF16)


def _seg_conv(xe, n, conv_w, conv_b):
    c = xe.shape[1]
    tiles = [xe[SUBLANES * i:SUBLANES * (i + 1)] for i in range(xe.shape[0] // SUBLANES)]
    sub = lax.broadcasted_iota(jnp.int32, (SUBLANES, c), 0)
    w = [conv_w[k:k + 1, :] for k in range(LRU_CONV_W)]
    out = []
    for blk in range(n // SEG_BLOCK):
        base = 2 + SUBLANES * blk
        cur = tiles[base:base + SUBLANES]
        def down(j):
            return jnp.where(sub == 0, pltpu.roll(tiles[base - SUBLANES + j], 1, 0), pltpu.roll(cur[j], 1, 0))
        back1, back2 = down(SUBLANES - 1), down(SUBLANES - 2)
        ahead = jnp.where(sub == SUBLANES - 1, pltpu.roll(tiles[base + SUBLANES], SUBLANES - 1, 0),
                          pltpu.roll(cur[0], SUBLANES - 1, 0))
        for j in range(SUBLANES):
            m2 = cur[j - 2] if j >= 2 else (back2 if j == 0 else back1)
            m1 = cur[j - 1] if j >= 1 else back1
            p1 = cur[j + 1] if j + 1 < SUBLANES else ahead
            out.append(conv_b + w[0] * m2 + w[1] * m1 + w[2] * cur[j] + w[3] * p1)
    return jnp.concatenate(out, axis=0)


def _lru_coeffs(xc, wcat_ref, bcat, sp, d):
    xcb = xc.astype(BF16)
    rs, gs = [], []
    for blk in range(LRU_CG // LRU_BLOCK_W):
        bl = slice(LRU_BLOCK_W * blk, LRU_BLOCK_W * (blk + 1))
        z = jnp.dot(xcb[:, bl], wcat_ref[d, blk], preferred_element_type=F32)
        rs.append(z[:, :LRU_BLOCK_W])
        gs.append(z[:, LRU_BLOCK_W:])
    r = jax.nn.sigmoid(jnp.concatenate(rs, axis=-1) + bcat[2 * d:2 * d + 1, :])
    gate = jax.nn.sigmoid(jnp.concatenate(gs, axis=-1) + bcat[2 * d + 1:2 * d + 2, :])
    log_a = (-LRU_C * r) * sp[d:d + 1, :]
    a = jnp.exp(log_a)
    b = jnp.sqrt(-jnp.tanh(log_a) * (a * a + 1.0)) * gate * xc
    return a, b


def _seg_scan(a, b, carry, reverse, want_h=True):
    n, c = a.shape
    nb = n // SEG_BLOCK
    sub = lax.broadcasted_iota(jnp.int32, (SUBLANES, c), 0)
    steps = range(SUBLANES - 1, -1, -1) if reverse else range(SUBLANES)
    hs = [None] * (nb * SUBLANES)
    for blk in (range(nb - 1, -1, -1) if reverse else range(nb)):
        tile = lambda x, j: x[blk * SEG_BLOCK + j * SUBLANES:blk * SEG_BLOCK + (j + 1) * SUBLANES]
        acc_a, acc_h = {}, {}
        prev = None
        for j in steps:
            aj, bj = tile(a, j), tile(b, j)
            acc_h[j] = bj if prev is None else aj * acc_h[prev] + bj
            acc_a[j] = aj if prev is None else aj * acc_a[prev]
            prev = j
        pa, ph = acc_a[prev], acc_h[prev]
        for s in (1, 2, 4):
            shift = SUBLANES - s if reverse else s
            m = (sub < SUBLANES - s) if reverse else (sub >= s)
            ph = jnp.where(m, pa * pltpu.roll(ph, shift, 0) + ph, ph)
            pa = jnp.where(m, pa * pltpu.roll(pa, shift, 0), pa)
        end_state = pa * carry + ph
        if reverse:
            seg_in = jnp.where(sub == SUBLANES - 1, carry, pltpu.roll(end_state, SUBLANES - 1, 0))
            carry = end_state[0:1]
        else:
            seg_in = jnp.where(sub == 0, carry, pltpu.roll(end_state, 1, 0))
            carry = end_state[SUBLANES - 1:SUBLANES]
        if want_h:
            for j in steps:
                hs[blk * SUBLANES + j] = acc_h[j] + acc_a[j] * seg_in
    return (jnp.concatenate(hs, axis=0) if want_h else None), carry


def _lru_kernel(x_ref, ctx_ref, shx_ref, scx_ref, shc_ref, scc_ref, g_ref, perm_ref, wxr_ref, wgx_ref,
                cw_ref, cb_ref, wcat_ref, bcat_ref, lam_ref, y_ref,
                xn_s, cn_s, xr_s, cr_s, xc_s, gg_s, hf_s, yp_s):
    t = x_ref.shape[0]
    nctx = ctx_ref.shape[0]
    cg = y_ref.shape[1]
    perm = perm_ref[...]

    def permuted_norm(rows_f32, shift, scale):
        xn = _modulate(rows_f32, g_ref[...], shift, scale).astype(BF16)
        parts = [jnp.dot(perm, xn[LRU_CHUNK * i:LRU_CHUNK * (i + 1)], preferred_element_type=F32)
                 for i in range(xn.shape[0] // LRU_CHUNK)]
        return jnp.concatenate(parts, axis=0).astype(BF16)

    @pl.when(pl.program_id(1) == 0)
    def _():
        def body(i, _):
            r0 = pl.multiple_of(i * TOK_TILE, TOK_TILE)
            xn_s[pl.ds(r0, TOK_TILE), :] = permuted_norm(x_ref[pl.ds(r0, TOK_TILE), :], shx_ref[0], scx_ref[0])
            return 0
        lax.fori_loop(0, t // TOK_TILE, body, 0)
        cn_s[...] = permuted_norm(ctx_ref[...], shc_ref[0], scc_ref[0])

    xr_s[0:LRU_HALO_FRONT, :] = jnp.zeros((LRU_HALO_FRONT, cg), F32)
    xr_s[t + LRU_HALO_FRONT:, :] = jnp.zeros((LRU_HALO_BACK, cg), F32)
    cr_s[0:LRU_HALO_FRONT, :] = jnp.zeros((LRU_HALO_FRONT, cg), F32)
    cr_s[nctx + LRU_HALO_FRONT:, :] = jnp.zeros((LRU_HALO_BACK, cg), F32)

    def proj(i, _):
        r0 = pl.multiple_of(i * TOK_TILE, TOK_TILE)
        xn = xn_s[pl.ds(r0, TOK_TILE), :]
        xr_s[pl.ds(r0 + LRU_HALO_FRONT, TOK_TILE), :] = jnp.dot(xn, wxr_ref[...], preferred_element_type=F32)
        gg_s[pl.ds(r0, TOK_TILE), :] = jax.nn.gelu(
            jnp.dot(xn, wgx_ref[...], preferred_element_type=F32))
        return 0
    lax.fori_loop(0, t // TOK_TILE, proj, 0)
    cr_s[LRU_HALO_FRONT:LRU_HALO_FRONT + nctx, :] = jnp.dot(cn_s[...], wxr_ref[...],
                                                            preferred_element_type=F32)

    conv_w = cw_ref[...]
    conv_b = cb_ref[...]
    bcat = bcat_ref[...]
    sp = _softplus(-lam_ref[...])
    chunk_ext = LRU_CHUNK + LRU_HALO_FRONT + LRU_HALO_BACK
    conv = functools.partial(_seg_conv, n=LRU_CHUNK, conv_w=conv_w, conv_b=conv_b)
    coeffs = functools.partial(_lru_coeffs, wcat_ref=wcat_ref, bcat=bcat, sp=sp)
    n_lat = t // LRU_CHUNK
    n_ctx = nctx // LRU_CHUNK
    ctx_xc = [conv(cr_s[i * LRU_CHUNK:i * LRU_CHUNK + chunk_ext, :]) for i in range(n_ctx)]

    carry = jnp.zeros((1, cg), F32)
    for i in range(n_ctx):
        a, b = coeffs(ctx_xc[i], d=0)
        _, carry = _seg_scan(a, b, carry, reverse=False, want_h=False)

    def fwd(i, carry):
        r0 = pl.multiple_of(i * LRU_CHUNK, LRU_CHUNK)
        xc = conv(xr_s[pl.ds(r0, chunk_ext), :])
        xc_s[pl.ds(r0, LRU_CHUNK), :] = xc
        a, b = coeffs(xc, d=0)
        h, carry = _seg_scan(a, b, carry, reverse=False)
        hf_s[pl.ds(r0, LRU_CHUNK), :] = h
        return carry
    lax.fori_loop(0, n_lat, fwd, carry, unroll=LRU_UNROLL)

    carry = jnp.zeros((1, cg), F32)
    for i in range(n_ctx - 1, -1, -1):
        a, b = coeffs(ctx_xc[i], d=1)
        _, carry = _seg_scan(a, b, carry, reverse=True, want_h=False)

    def bwd(j, carry):
        r0 = pl.multiple_of((n_lat - 1 - j) * LRU_CHUNK, LRU_CHUNK)
        rows = pl.ds(r0, LRU_CHUNK)
        a, b = coeffs(xc_s[rows, :], d=1)
        h, carry = _seg_scan(a, b, carry, reverse=True)
        yp_s[rows, :] = ((hf_s[rows, :] + h) * gg_s[rows, :]).astype(BF16)
        return carry
    lax.fori_loop(0, n_lat, bwd, carry, unroll=LRU_UNROLL)

    def unpermute(i, _):
        r0 = pl.multiple_of(i * TOK_TILE, TOK_TILE)
        for k in range(TOK_TILE // LRU_CHUNK):
            rows = pl.ds(r0 + k * LRU_CHUNK, LRU_CHUNK)
            y_ref[rows, :] = jnp.dot(perm, yp_s[rows, :], preferred_element_type=F32).astype(BF16)
        return 0
    lax.fori_loop(0, t // TOK_TILE, unpermute, 0)


def _lru_call(x, ctx, mods3, g_mix, w_in, conv_w, conv_b, wcat, bcat, lam):
    b, t, d = x.shape
    nctx = ctx.shape[1]
    cg = LRU_CG
    ng = D_RNN // cg
    ctx_row = b * N_MOD
    mod = lambda k: pl.BlockSpec((1, 1, D_MODEL), lambda bi, g: (bi * N_MOD + k, 0, 0))
    modc = lambda k: pl.BlockSpec((1, 1, D_MODEL), lambda bi, g: (ctx_row + k, 0, 0))
    in_specs = [
        pl.BlockSpec((None, t, d), lambda bi, g: (bi, 0, 0)),
        pl.BlockSpec((None, nctx, d), lambda bi, g: (bi, 0, 0)),
        mod(0), mod(1), modc(0), modc(1),
        pl.BlockSpec((1, D_MODEL), lambda bi, g: (0, 0)),
        pl.BlockSpec((LRU_CHUNK, LRU_CHUNK), lambda bi, g: (0, 0)),
        pl.BlockSpec((D_MODEL, cg), lambda bi, g: (0, COL_XR * ng + g)),
        pl.BlockSpec((D_MODEL, cg), lambda bi, g: (0, COL_GX * ng + g)),
        pl.BlockSpec((LRU_CONV_W, cg), lambda bi, g: (0, g)),
        pl.BlockSpec((1, cg), lambda bi, g: (0, g)),
        pl.BlockSpec((2, cg // LRU_BLOCK_W, LRU_BLOCK_W, 2 * LRU_BLOCK_W), lambda bi, g: (0, g, 0, 0)),
        pl.BlockSpec((4, cg), lambda bi, g: (0, g)),
        pl.BlockSpec((2, cg), lambda bi, g: (0, g)),
    ]
    halo = LRU_HALO_FRONT + LRU_HALO_BACK
    scratch = [
        pltpu.VMEM((t, d), BF16), pltpu.VMEM((nctx, d), BF16),
        pltpu.VMEM((t + halo, cg), F32), pltpu.VMEM((nctx + halo, cg), F32),
        pltpu.VMEM((t, cg), F32), pltpu.VMEM((t, cg), F32), pltpu.VMEM((t, cg), F32),
        pltpu.VMEM((t, cg), BF16),
    ]
    bufs = [((t, d), F32, 2), ((nctx, d), F32, 2), ((D_MODEL, cg), BF16, 4), ((t, cg), BF16, 3),
            ((t, d), BF16, 1), ((nctx, d), BF16, 1), ((t + halo, cg), F32, 4),
            ((nctx + halo, cg), F32, 1)]
    return pl.pallas_call(
        _lru_kernel,
        grid=(b, ng),
        in_specs=in_specs,
        out_specs=pl.BlockSpec((None, t, cg), lambda bi, g: (bi, 0, g)),
        out_shape=jax.ShapeDtypeStruct((b, t, D_RNN), BF16),
        scratch_shapes=scratch,
        compiler_params=_params(("arbitrary", "arbitrary"), bufs),
        name="rglru",
    )(x, ctx, mods3, mods3, mods3, mods3, g_mix, _seg_permutation(LRU_CHUNK), w_in, w_in,
      conv_w, conv_b, wcat, bcat, lam)


NT_DIMS = (((1,), (1,)), ((), ()))
KEY_ROWS_PER_TILE = LANES // GRID_W


def _attn_group_geometry(g, rows):
    r0 = g * ATTN_GROUP
    starts = [min(max(r0 + o - NA_ROWS // 2, 0), rows - NA_ROWS) for o in range(ATTN_GROUP)]
    key_row0 = starts[0]
    n_key_rows = -(-(starts[-1] + NA_ROWS - key_row0) // KEY_ROWS_PER_TILE) * KEY_ROWS_PER_TILE
    desc = tuple((starts[o] - key_row0, key_row0 - (r0 + o)) for o in range(ATTN_GROUP))
    return key_row0, n_key_rows, desc


BIAS_BOTH, BIAS_RIGHT_ONLY, BIAS_LEFT_ONLY = range(3)


def _bias_tile_index(off, base, tile):
    left = KEY_ROWS_PER_TILE * tile
    in_window = lambda j: off <= j < off + NA_ROWS
    dr_left = base + left + (NA_ROWS - 1)
    if in_window(left) and in_window(left + 1):
        return BIAS_BOTH, dr_left
    if in_window(left + 1):
        return BIAS_RIGHT_ONLY, dr_left + 1
    return BIAS_LEFT_ONLY, dr_left


def _attn_kernel(qr_ref, qp_ref, k_ref, v_ref, kc_ref, vc_ref, bias_ref, y_ref, qs_s, qps_s):
    t = k_ref.shape[0]
    rows = t // GRID_W
    lane = lax.broadcasted_iota(jnp.int32, (rows, GRID_W, LANES), 2)
    head_a = lane < HEAD_DIM

    def stack_heads(q_ref, dst):
        q3 = q_ref[...].reshape(rows, GRID_W, LANES)
        zero = jnp.zeros_like(q3)
        dst[...] = jnp.concatenate([jnp.where(head_a, q3, zero), jnp.where(head_a, zero, q3)], axis=1)
    stack_heads(qr_ref, qs_s)
    stack_heads(qp_ref, qps_s)

    kc = kc_ref[...]
    vc = vc_ref[...]
    out_a = lax.broadcasted_iota(jnp.int32, (GRID_W, LANES), 1) < HEAD_DIM
    rows_stacked = 2 * GRID_W

    def lane_tiles(arrays):
        return [a[:, LANES * i:LANES * (i + 1)] for a in arrays for i in range(a.shape[1] // LANES)]

    for g in range(rows // ATTN_GROUP):
        r0 = g * ATTN_GROUP
        key_row0, n_key_rows, desc = _attn_group_geometry(g, rows)
        k_sup = k_ref[key_row0 * GRID_W:(key_row0 + n_key_rows) * GRID_W, :]
        v_sup = v_ref[key_row0 * GRID_W:(key_row0 + n_key_rows) * GRID_W, :]
        n_tiles = n_key_rows // KEY_ROWS_PER_TILE
        q_grp = qs_s[r0:r0 + ATTN_GROUP].reshape(ATTN_GROUP * rows_stacked, LANES)
        qp_grp = qps_s[r0:r0 + ATTN_GROUP].reshape(ATTN_GROUP * rows_stacked, LANES)
        s_lat = lax.dot_general(q_grp, k_sup, NT_DIMS, preferred_element_type=F32)
        s_ctx = lax.dot_general(qp_grp, kc, NT_DIMS, preferred_element_type=F32)
        p_rows, pc_rows, denoms = [], [], []
        for o in range(ATTN_GROUP):
            off, base = desc[o]
            lo, hi = off // KEY_ROWS_PER_TILE, -(-(off + NA_ROWS) // KEY_ROWS_PER_TILE)
            rsl = slice(o * rows_stacked, (o + 1) * rows_stacked)
            bias = []
            for tile in range(lo, hi):
                kind, dr = _bias_tile_index(off, base, tile)
                bias.append(jnp.concatenate([bias_ref[0, kind, dr], bias_ref[1, kind, dr]], axis=0))
            s_o = s_lat[rsl, lo * LANES:hi * LANES] + jnp.concatenate(bias, axis=-1)
            sc_o = s_ctx[rsl]
            m = jnp.max(functools.reduce(jnp.maximum, lane_tiles([s_o, sc_o])), axis=-1, keepdims=True)
            p_o = jnp.exp(s_o - m)
            pc_o = jnp.exp(sc_o - m)
            denoms.append(jnp.sum(functools.reduce(jnp.add, lane_tiles([p_o, pc_o])), axis=-1, keepdims=True))
            pieces = [p_o.astype(BF16)]
            if lo > 0:
                pieces.insert(0, jnp.zeros((rows_stacked, lo * LANES), BF16))
            if hi < n_tiles:
                pieces.append(jnp.zeros((rows_stacked, (n_tiles - hi) * LANES), BF16))
            p_rows.append(jnp.concatenate(pieces, axis=-1) if len(pieces) > 1 else pieces[0])
            pc_rows.append(pc_o.astype(BF16))
        o_grp = (jnp.dot(jnp.concatenate(p_rows, axis=0), v_sup, preferred_element_type=F32)
                 + jnp.dot(jnp.concatenate(pc_rows, axis=0), vc, preferred_element_type=F32))
        o_grp = o_grp / jnp.concatenate(denoms, axis=0)
        for o in range(ATTN_GROUP):
            blk = o_grp[o * rows_stacked:(o + 1) * rows_stacked]
            y_ref[(r0 + o) * GRID_W:(r0 + o + 1) * GRID_W, :] = jnp.where(
                out_a, blk[:GRID_W], blk[GRID_W:]).astype(BF16)


def _attn_call(q_rot, q_plain, k_rot, v, kc, vc, bias_tab):
    b, t, _ = q_rot.shape
    nctx = kc.shape[1]
    rows = t // GRID_W
    npair = D_ATT // LANES
    tok = pl.BlockSpec((None, t, LANES), lambda bi, hp: (bi, 0, hp))
    ctx = pl.BlockSpec((None, nctx, LANES), lambda bi, hp: (bi, 0, hp))
    bias_shape = (LANES // HEAD_DIM,) + bias_tab.shape[1:]
    bufs = [((t, LANES), BF16, 10), ((nctx, LANES), BF16, 4), (bias_shape, F32, 2),
            ((rows, 2 * GRID_W, LANES), BF16, 2)]
    return pl.pallas_call(
        _attn_kernel,
        grid=(b, npair),
        in_specs=[tok, tok, tok, tok, ctx, ctx,
                  pl.BlockSpec(bias_shape, lambda bi, hp: (hp, 0, 0, 0, 0))],
        out_specs=tok,
        out_shape=jax.ShapeDtypeStruct((b, t, D_ATT), BF16),
        scratch_shapes=[pltpu.VMEM((rows, 2 * GRID_W, LANES), BF16),
                        pltpu.VMEM((rows, 2 * GRID_W, LANES), BF16)],
        compiler_params=_params(("arbitrary", "arbitrary"), bufs),
        name="nbr_attn",
    )(q_rot, q_plain, k_rot, v, kc, vc, bias_tab)


def _bias_table(rpb):
    c = np.arange(GRID_W)
    col_start = np.clip(c - NA_COLS // 2, 0, GRID_W - NA_COLS)
    in_win = (c[None, :] >= col_start[:, None]) & (c[None, :] < col_start[:, None] + NA_COLS)
    dc = np.clip(c[None, :] - c[:, None], -(NA_COLS - 1), NA_COLS - 1) + (NA_COLS - 1)
    select = (dc[None] == np.arange(2 * NA_COLS - 1)[:, None, None]).astype(np.float32)
    toe = jnp.einsum("hdm,mqk->hdqk", rpb, jnp.asarray(select), precision=lax.Precision.HIGHEST)
    toe = jnp.where(jnp.asarray(in_win), toe, NEG_INF)
    neg = jnp.full_like(toe, NEG_INF)
    nxt = jnp.concatenate([toe[:, 1:], neg[:, :1]], axis=1)
    both = jnp.concatenate([toe, nxt], axis=-1)
    right_only = jnp.concatenate([neg, toe], axis=-1)
    left_only = jnp.concatenate([toe, neg], axis=-1)
    return jnp.stack([both, right_only, left_only], axis=1)


def _merge_kernel(x_ref, yr_ref, yn_ref, sh_ref, sc_ref, gate_ref, g_ref,
                  wgr_ref, wgn_ref, wr_ref, wn_ref, wo_ref, o_ref):
    x = x_ref[...]
    xn = _modulate(x, g_ref[...], sh_ref[0], sc_ref[0]).astype(BF16)
    z_r = jnp.dot(yr_ref[...], wr_ref[...], preferred_element_type=F32)
    z_n = jnp.dot(yn_ref[...], wn_ref[...], preferred_element_type=F32)
    g_r = jax.nn.sigmoid(jnp.dot(xn, wgr_ref[...], preferred_element_type=F32))
    g_n = jax.nn.sigmoid(jnp.dot(xn, wgn_ref[...], preferred_element_type=F32))
    merged = g_r * z_r + g_n * z_n
    o_ref[...] = x + gate_ref[0] * jnp.dot(merged.astype(BF16), wo_ref[...], preferred_element_type=F32)


def _merge_call(x, y_rnn, y_na, mods3, g_mix, w_in, w_rnn_out, w_na_out, w_out):
    b, t, d = x.shape
    tm = TOK_TILE
    tok = pl.BlockSpec((None, tm, d), lambda bi, i: (bi, i, 0))
    mod = lambda k: pl.BlockSpec((1, 1, D_MODEL), lambda bi, i: (bi * N_MOD + k, 0, 0))
    wblk = lambda c: pl.BlockSpec((D_MODEL, D_MODEL), lambda bi, i: (0, c))
    bufs = [((tm, d), F32, 4), ((tm, d), BF16, 4), ((D_MODEL, D_MODEL), BF16, 10), ((tm, d), F32, 4)]
    return pl.pallas_call(
        _merge_kernel,
        grid=(b, t // tm),
        in_specs=[tok, tok, tok, mod(0), mod(1), mod(2),
                  pl.BlockSpec((1, D_MODEL), lambda bi, i: (0, 0)),
                  wblk(COL_MR), wblk(COL_MN), wblk(0), wblk(0), wblk(0)],
        out_specs=tok,
        out_shape=jax.ShapeDtypeStruct((b, t, d), F32),
        compiler_params=_params(("arbitrary", "arbitrary"), bufs),
        name="merge",
    )(x, y_rnn, y_na, mods3, mods3, mods3, g_mix, w_in, w_in, w_rnn_out, w_na_out, w_out)


def _ffn_kernel(x_ref, xp_ref, xnx_ref, sh_ref, sc_ref, gate_ref, g_ref, wup_ref, cw_ref, cb_ref,
                wdn_ref, o_ref, u_s, act_s):
    tm = x_ref.shape[0]
    i = pl.program_id(1)
    x = x_ref[...]
    xe = jnp.concatenate([xp_ref[...], x, xnx_ref[...]], axis=0)
    xn = _modulate(xe, g_ref[...], sh_ref[0], sc_ref[0])
    row = lax.broadcasted_iota(jnp.int32, (tm + 2 * HALO, 1), 0)
    outside = ((row < HALO) & (i == 0)) | ((row >= tm + HALO) & (i == pl.num_programs(1) - 1))
    xn = jnp.where(outside, 0.0, xn).astype(BF16)
    cw = cw_ref[...]
    cb = cb_ref[...]
    for c in range(D_FF // FF_CHUNK):
        halves = []
        for half in range(2):
            sl = slice(half * D_FF + c * FF_CHUNK, half * D_FF + (c + 1) * FF_CHUNK)
            u_s[half] = jnp.dot(xn, wup_ref[:, sl], preferred_element_type=F32)
            acc = cb[:, sl]
            for k in range(FFN_CONV_W):
                acc = acc + cw[k:k + 1, sl] * u_s[half, pl.ds(HALO - FFN_CONV_LEFT + k, tm), :]
            halves.append(acc)
        a, g = halves
        act_s[:, c * FF_CHUNK:(c + 1) * FF_CHUNK] = ((a * jax.nn.sigmoid(a)) * g).astype(BF16)
    o_ref[...] = x + gate_ref[0] * jnp.dot(act_s[...], wdn_ref[...], preferred_element_type=F32)


def _ffn_call(x1, mods3, g_ffn, w_up, conv_w, conv_b, w_down):
    b, t, d = x1.shape
    tm = TOK_TILE
    hb = tm // HALO
    n_halo_blocks = t // HALO
    tok = pl.BlockSpec((None, tm, d), lambda bi, i: (bi, i, 0))
    prev = pl.BlockSpec((None, HALO, d), lambda bi, i: (bi, jnp.maximum(i * hb - 1, 0), 0))
    nxt = pl.BlockSpec((None, HALO, d), lambda bi, i: (bi, jnp.minimum((i + 1) * hb, n_halo_blocks - 1), 0))
    mod = lambda k: pl.BlockSpec((1, 1, D_MODEL), lambda bi, i: (bi * N_MOD + k, 0, 0))
    const = lambda shape: pl.BlockSpec(shape, lambda bi, i: (0,) * len(shape),
                                       pipeline_mode=pl.Buffered(1))
    bufs = [((tm, d), F32, 4), ((D_MODEL, 2 * D_FF), BF16, 1), ((D_FF, D_MODEL), BF16, 1),
            ((2, tm + 2 * HALO, FF_CHUNK), F32, 1), ((tm, D_FF), BF16, 1),
            ((tm + 2 * HALO, d), F32, 2)]
    return pl.pallas_call(
        _ffn_kernel,
        grid=(b, t // tm),
        in_specs=[tok, prev, nxt, mod(3), mod(4), mod(5), const((1, D_MODEL)),
                  const((D_MODEL, 2 * D_FF)), const((FFN_CONV_W, 2 * D_FF)), const((1, 2 * D_FF)),
                  const((D_FF, D_MODEL))],
        out_specs=tok,
        out_shape=jax.ShapeDtypeStruct((b, t, d), F32),
        scratch_shapes=[pltpu.VMEM((2, tm + 2 * HALO, FF_CHUNK), F32), pltpu.VMEM((tm, D_FF), BF16)],
        compiler_params=_params(("arbitrary", "arbitrary"), bufs),
        name="conv_ffn",
    )(x1, x1, x1, mods3, mods3, mods3, g_ffn, w_up, conv_w, conv_b, w_down)


def _rope_tables(t):
    n = HEAD_DIM // 4
    freq = (ROPE_BASE ** (-np.arange(n, dtype=np.float32) / n)).astype(np.float32)
    pos = np.arange(t)
    ang_r = (pos // GRID_W).astype(np.float32)[:, None] * freq
    ang_c = (pos % GRID_W).astype(np.float32)[:, None] * freq
    cos = np.concatenate([np.cos(ang_r)] * 2 + [np.cos(ang_c)] * 2, axis=-1)
    sin = np.concatenate([-np.sin(ang_r), np.sin(ang_r), -np.sin(ang_c), np.sin(ang_c)], axis=-1)
    reps = LANES // HEAD_DIM
    return (jnp.asarray(np.tile(cos, (1, reps)), F32), jnp.asarray(np.tile(sin, (1, reps)), F32))


def kernel(x, c, ctx, c_ctx, w_mod, b_mod, norm_mix_g, norm_ffn_g, w_in, lru_conv_w, lru_conv_b, lru_wa, lru_ba, lru_wx, lru_bx, lru_lambda, q_norm_g, k_norm_g, na_rpb, w_rnn_out, w_na_out, w_out, w_up, ffn_conv_w, ffn_conv_b, w_down):
    depth = w_mod.shape[0]
    assert depth == 1, "single-layer block"
    b, t, d = x.shape
    assert d == D_MODEL and b < MOD_ROWS and t % TOK_TILE == 0 and t % GRID_W == 0
    l = 0

    cc = jnp.zeros((MOD_ROWS, d), F32).at[:b].set(c).at[b].set(c_ctx)
    w_in_b = w_in[l].astype(BF16)
    g_mix = norm_mix_g[l][None, :]
    g_ffn = norm_ffn_g[l][None, :]
    q_gain = jnp.tile(q_norm_g[l], NA_HEADS)[None, :]
    k_gain = jnp.tile(k_norm_g[l], NA_HEADS)[None, :]
    head_of = np.arange(MXU_DIM) // HEAD_DIM
    e_mat = jnp.asarray(np.where(head_of[:, None] == head_of[None, :], 1.0 / HEAD_DIM, 0.0), BF16)
    cos_t, sin_t = _rope_tables(t)
    wcat = jnp.concatenate([lru_wa[l], lru_wx[l]], axis=-1).astype(BF16)
    bcat = jnp.stack([lru_ba[l, 0], lru_bx[l, 0], lru_ba[l, 1], lru_bx[l, 1]])
    bias_tab = _bias_table(na_rpb[l])

    mods = _mod_call(cc, w_mod[l], b_mod[l][None, :])
    mods3 = mods.reshape(MOD_ROWS * N_MOD, 1, d)

    k_rot, v, q_rot, q_plain = _qkv_call(x, mods3, 0, g_mix, w_in_b, k_gain, q_gain, e_mat,
                                         cos_t, sin_t, latent=True)
    kc, vc = _qkv_call(ctx, mods3, b, g_mix, w_in_b, k_gain, None, e_mat, None, None, latent=False)
    y_rnn = _lru_call(x, ctx, mods3, g_mix, w_in_b, lru_conv_w[l], lru_conv_b[l][None, :],
                      wcat, bcat, lru_lambda[l])
    y_na = _attn_call(q_rot, q_plain, k_rot, v, kc, vc, bias_tab)
    x1 = _merge_call(x, y_rnn, y_na, mods3, g_mix, w_in_b, w_rnn_out[l].astype(BF16),
                     w_na_out[l].astype(BF16), w_out[l].astype(BF16))
    return _ffn_call(x1, mods3, g_ffn, w_up[l].astype(BF16), ffn_conv_w[l], ffn_conv_b[l][None, :],
                     w_down[l].astype(BF16))
```

```python
import functools

import jax
import jax.numpy as jnp
import numpy as np
from jax import lax
from jax.experimental import pallas as pl
from jax.experimental.pallas import tpu as pltpu

F32 = jnp.float32
BF16 = jnp.bfloat16

D_MODEL = 1024
GRID_W = 64
D_RNN = D_MODEL
LRU_BLOCKS = 8
LRU_BLOCK_W = D_RNN // LRU_BLOCKS
LRU_CONV_W = 4
LRU_CONV_LEFT = 2
LRU_C = 8.0
NA_HEADS = 16
HEAD_DIM = 64
D_ATT = NA_HEADS * HEAD_DIM
NA_ROWS = 8
NA_COLS = 16
ROPE_BASE = 10000.0
D_FF = ((8 * D_MODEL // 3 + 127) // 128) * 128
FFN_CONV_W = 3
FFN_CONV_LEFT = 1
N_MOD = 6
EPS = 1e-6
NEG_INF = -1e30

COL_XR, COL_K, COL_V, COL_GX, COL_Q, COL_MR, COL_MN = range(7)

LANES = 128
SUBLANES = 8
MXU_DIM = 256
VMEM_LIMIT_CAP = 60000 * 1024
VMEM_TEMP_HEADROOM = 12 * 1024 * 1024

MOD_ROWS = 16
TOK_TILE = 512
LRU_CG = 256
LRU_CHUNK = 128
LRU_UNROLL = 4
HALO = SUBLANES
FF_CHUNK = MXU_DIM
FFN_SUBTILES = 2
QKV_SUBTILES = 2
PERM_ROWS = 128
ATTN_GROUP = 4


def _nbytes(shape, dtype):
    n = 1
    for s in shape:
        n *= s
    return n * jnp.dtype(dtype).itemsize


def _vmem_limit(buffers):
    need = sum(_nbytes(s, d) * c for s, d, c in buffers) + VMEM_TEMP_HEADROOM
    return int(min(need, VMEM_LIMIT_CAP))


def _params(semantics, buffers):
    return pltpu.CompilerParams(dimension_semantics=semantics,
                                vmem_limit_bytes=_vmem_limit(buffers))


def _modulate(x, g, shift, scale):
    ms = jnp.mean(x * x, axis=-1, keepdims=True)
    y = x * lax.rsqrt(ms + EPS)
    return (y * g) * (1.0 + scale) + shift


def _mod_kernel(cc_ref, w_ref, b_ref, o_ref):
    s = cc_ref[...]
    s = s * jax.nn.sigmoid(s)
    o_ref[...] = jnp.dot(s.astype(BF16), w_ref[...].astype(BF16),
                         preferred_element_type=F32) + b_ref[...]


def _mod_call(cc, w_mod, b_mod):
    n = w_mod.shape[1]
    tn = D_MODEL
    bufs = [((MOD_ROWS, D_MODEL), F32, 2), ((D_MODEL, tn), F32, 2), ((MOD_ROWS, tn), F32, 2)]
    return pl.pallas_call(
        _mod_kernel,
        grid=(n // tn,),
        in_specs=[pl.BlockSpec((MOD_ROWS, D_MODEL), lambda j: (0, 0)),
                  pl.BlockSpec((D_MODEL, tn), lambda j: (0, j)),
                  pl.BlockSpec((1, tn), lambda j: (0, j))],
        out_specs=pl.BlockSpec((MOD_ROWS, tn), lambda j: (0, j)),
        out_shape=jax.ShapeDtypeStruct((MOD_ROWS, n), F32),
        compiler_params=_params(("arbitrary",), bufs),
        name="adaln_mod",
    )(cc, w_mod, b_mod)


def _head_rms(t, gain, e):
    sq = t * t
    hi = sq.astype(BF16)
    lo = (sq - hi.astype(F32)).astype(BF16)
    ms = jnp.dot(hi, e, preferred_element_type=F32) + jnp.dot(lo, e, preferred_element_type=F32)
    return (t * lax.rsqrt(ms + EPS)) * gain


def _rope(y, cos, sin_signed, first_half):
    partner = jnp.where(first_half, pltpu.roll(y, LANES - 16, 1), pltpu.roll(y, 16, 1))
    return y * cos + partner * sin_signed


def _qkv_kernel(*refs, latent):
    if latent:
        (x_ref, sh_ref, sc_ref, g_ref, wk_ref, wv_ref, wq_ref, kg_ref, qg_ref, e_ref,
         cos_ref, sin_ref, k_out, v_out, qr_out, qp_out) = refs
    else:
        (x_ref, sh_ref, sc_ref, g_ref, wk_ref, wv_ref, kg_ref, e_ref, k_out, v_out) = refs
    e = e_ref[...]
    scale = HEAD_DIM ** -0.5
    tm = x_ref.shape[0]
    ts = tm // QKV_SUBTILES
    for s in range(QKV_SUBTILES):
        rows = slice(s * ts, (s + 1) * ts)
        xn = _modulate(x_ref[rows, :], g_ref[...], sh_ref[0], sc_ref[0]).astype(BF16)
        if latent:
            cos = cos_ref[rows, :]
            sin = sin_ref[rows, :]
            lane = lax.broadcasted_iota(jnp.int32, cos.shape, 1)
            first_half = (lane % 32) < 16
        for j in range(D_ATT // MXU_DIM):
            sl = slice(MXU_DIM * j, MXU_DIM * (j + 1))
            v_out[rows, sl] = jnp.dot(xn, wv_ref[:, sl], preferred_element_type=F32).astype(BF16)
            k = _head_rms(jnp.dot(xn, wk_ref[:, sl], preferred_element_type=F32), kg_ref[:, sl], e)
            if not latent:
                k_out[rows, sl] = k.astype(BF16)
                continue
            q = _head_rms(jnp.dot(xn, wq_ref[:, sl], preferred_element_type=F32), qg_ref[:, sl], e)
            qp_out[rows, sl] = (q * scale).astype(BF16)
            for h in range(MXU_DIM // LANES):
                hl = slice(LANES * h, LANES * (h + 1))
                ol = slice(MXU_DIM * j + LANES * h, MXU_DIM * j + LANES * (h + 1))
                k_out[rows, ol] = _rope(k[:, hl], cos, sin, first_half).astype(BF16)
                qr_out[rows, ol] = (_rope(q[:, hl], cos, sin, first_half) * scale).astype(BF16)


def _qkv_call(x, mods3, mod_row0, g_mix, w_in, k_gain, q_gain, e_mat, cos_t, sin_t, latent):
    b, t, d = x.shape
    tm = min(TOK_TILE, t)
    nt = t // tm
    wblk = lambda c: pl.BlockSpec((D_MODEL, D_MODEL), lambda bi, i: (0, c))
    vec = pl.BlockSpec((1, D_MODEL), lambda bi, i: (0, 0))
    if latent:
        mod = lambda k: pl.BlockSpec((1, 1, D_MODEL), lambda bi, i: (bi * N_MOD + k, 0, 0))
    else:
        mod = lambda k: pl.BlockSpec((1, 1, D_MODEL), lambda bi, i: (mod_row0 * N_MOD + k, 0, 0))
    tok = pl.BlockSpec((None, tm, d), lambda bi, i: (bi, i, 0))
    in_specs = [tok, mod(0), mod(1), vec, wblk(COL_K), wblk(COL_V)]
    args = [x, mods3, mods3, g_mix, w_in, w_in]
    if latent:
        in_specs += [wblk(COL_Q), vec, vec]
        args += [w_in, k_gain, q_gain]
    else:
        in_specs += [vec]
        args += [k_gain]
    in_specs += [pl.BlockSpec((MXU_DIM, MXU_DIM), lambda bi, i: (0, 0))]
    args += [e_mat]
    n_out = 2
    if latent:
        rope_spec = pl.BlockSpec((tm, LANES), lambda bi, i: (i, 0))
        in_specs += [rope_spec, rope_spec]
        args += [cos_t, sin_t]
        n_out = 4
    out_sd = jax.ShapeDtypeStruct((b, t, D_ATT), BF16)
    bufs = [((tm, d), F32, 2), ((D_MODEL, D_MODEL), BF16, 6), ((tm, D_ATT), BF16, 2 * n_out),
            ((tm, LANES), F32, 4), ((tm, d), BF16, 1)]
    return pl.pallas_call(
        functools.partial(_qkv_kernel, latent=latent),
        grid=(b, nt),
        in_specs=in_specs,
        out_specs=[tok] * n_out,
        out_shape=[out_sd] * n_out,
        compiler_params=_params(("arbitrary", "arbitrary"), bufs),
        name="qkv_proj" if latent else "ctx_proj",
    )(*args)


def _softplus(x):
    return jnp.maximum(x, 0.0) + jnp.log1p(jnp.exp(-jnp.abs(x)))


SEG_BLOCK = SUBLANES * SUBLANES
LRU_HALO_FRONT = 2 * SUBLANES
LRU_HALO_BACK = SUBLANES


def _seg_permutation(n):
    p = np.arange(n)
    src = (p // SEG_BLOCK) * SEG_BLOCK + (p % SUBLANES) * SUBLANES + (p % SEG_BLOCK) // SUBLANES
    return jnp.asarray(src[:, None] == p[None, :], BF16)


def _seg_permute_rows(perm, x):
    n = perm.shape[0]
    parts = [jnp.dot(perm, x[n * i:n * (i + 1)], preferred_element_type=F32) for i in range(x.shape[0] // n)]
    return jnp.concatenate(parts, axis=0).astype(BF16)


def _seg_conv(x, before, after, taps, bias, left):
    n, c = x.shape
    right = len(taps) - 1 - left
    nb = n // SEG_BLOCK
    tiles = [x[SUBLANES * i:SUBLANES * (i + 1)] for i in range(n // SUBLANES)]
    sub = lax.broadcasted_iota(jnp.int32, (SUBLANES, c), 0)
    out = []
    for blk in range(nb):
        cur = tiles[SUBLANES * blk:SUBLANES * (blk + 1)]
        down, up = {}, {}
        for e in range(1, left + 1):
            j = SUBLANES - e
            prev = tiles[SUBLANES * (blk - 1) + j] if blk > 0 else before[left - e]
            down[j] = jnp.where(sub == 0, pltpu.roll(prev, 1, 0), pltpu.roll(cur[j], 1, 0))
        for e in range(right):
            nxt = tiles[SUBLANES * (blk + 1) + e] if blk + 1 < nb else after[e]
            up[e] = jnp.where(sub == SUBLANES - 1, pltpu.roll(nxt, SUBLANES - 1, 0),
                              pltpu.roll(cur[e], SUBLANES - 1, 0))
        for j in range(SUBLANES):
            acc = bias
            for k, wk in enumerate(taps):
                src = j + k - left
                if src < 0:
                    tile = down[src + SUBLANES]
                elif src >= SUBLANES:
                    tile = up[src - SUBLANES]
                else:
                    tile = cur[src]
                acc = acc + wk * tile
            out.append(acc)
    return jnp.concatenate(out, axis=0)


def _lru_coeffs(xc, wcat_ref, bcat, sp8, d):
    xcb = xc.astype(BF16)
    rs, gs = [], []
    for blk in range(LRU_CG // LRU_BLOCK_W):
        bl = slice(LRU_BLOCK_W * blk, LRU_BLOCK_W * (blk + 1))
        z = jnp.dot(xcb[:, bl], wcat_ref[d, blk], preferred_element_type=F32)
        rs.append(z[:, :LRU_BLOCK_W])
        gs.append(z[:, LRU_BLOCK_W:])
    r = jax.nn.sigmoid(jnp.concatenate(rs, axis=-1) + bcat[2 * d:2 * d + 1, :])
    gate = jax.nn.sigmoid(jnp.concatenate(gs, axis=-1) + bcat[2 * d + 1:2 * d + 2, :])
    a = jnp.exp(r * -sp8[d:d + 1, :])
    w = jnp.tanh(r * sp8[d:d + 1, :]) * (a * a + 1.0)
    root = jnp.where(w > 0.0, w * lax.rsqrt(w), 0.0)
    b = root * gate * xc
    return a, b


def _seg_scan(a, b, carry, reverse, want_h=True):
    n, c = a.shape
    nb = n // SEG_BLOCK
    sub = lax.broadcasted_iota(jnp.int32, (SUBLANES, c), 0)
    steps = range(SUBLANES - 1, -1, -1) if reverse else range(SUBLANES)
    hs = [None] * (nb * SUBLANES)
    for blk in (range(nb - 1, -1, -1) if reverse else range(nb)):
        tile = lambda x, j: x[blk * SEG_BLOCK + j * SUBLANES:blk * SEG_BLOCK + (j + 1) * SUBLANES]
        acc_a, acc_h = {}, {}
        prev = None
        for j in steps:
            aj, bj = tile(a, j), tile(b, j)
            acc_h[j] = bj if prev is None else aj * acc_h[prev] + bj
            acc_a[j] = aj if prev is None else aj * acc_a[prev]
            prev = j
        pa, ph = acc_a[prev], acc_h[prev]
        for s in (1, 2, 4):
            shift = SUBLANES - s if reverse else s
            m = (sub < SUBLANES - s) if reverse else (sub >= s)
            ph = jnp.where(m, pa * pltpu.roll(ph, shift, 0) + ph, ph)
            pa = jnp.where(m, pa * pltpu.roll(pa, shift, 0), pa)
        end_state = pa * carry + ph
        if reverse:
            seg_in = jnp.where(sub == SUBLANES - 1, carry, pltpu.roll(end_state, SUBLANES - 1, 0))
            carry = end_state[0:1]
        else:
            seg_in = jnp.where(sub == 0, carry, pltpu.roll(end_state, 1, 0))
            carry = end_state[SUBLANES - 1:SUBLANES]
        if want_h:
            for j in steps:
                hs[blk * SUBLANES + j] = acc_h[j] + acc_a[j] * seg_in
    return (jnp.concatenate(hs, axis=0) if want_h else None), carry


def _lru_kernel(x_ref, ctx_ref, shx_ref, scx_ref, shc_ref, scc_ref, g_ref, perm_ref, wxr_ref, wgx_ref,
                cw_ref, cb_ref, wcat_ref, bcat_ref, lam_ref, y_ref,
                xn_s, cn_s, xr_s, cr_s, xc_s, gg_s, hf_s, yp_s):
    t = x_ref.shape[0]
    nctx = ctx_ref.shape[0]
    cg = y_ref.shape[1]
    perm = perm_ref[...]

    def permuted_norm(rows_f32, shift, scale):
        return _seg_permute_rows(perm, _modulate(rows_f32, g_ref[...], shift, scale).astype(BF16))

    @pl.when(pl.program_id(1) == 0)
    def _():
        def body(i, _):
            r0 = pl.multiple_of(i * TOK_TILE, TOK_TILE)
            xn_s[pl.ds(r0, TOK_TILE), :] = permuted_norm(x_ref[pl.ds(r0, TOK_TILE), :], shx_ref[0], scx_ref[0])
            return 0
        lax.fori_loop(0, t // TOK_TILE, body, 0)
        cn_s[...] = permuted_norm(ctx_ref[...], shc_ref[0], scc_ref[0])

    xr_s[0:LRU_HALO_FRONT, :] = jnp.zeros((LRU_HALO_FRONT, cg), F32)
    xr_s[t + LRU_HALO_FRONT:, :] = jnp.zeros((LRU_HALO_BACK, cg), F32)
    cr_s[0:LRU_HALO_FRONT, :] = jnp.zeros((LRU_HALO_FRONT, cg), F32)
    cr_s[nctx + LRU_HALO_FRONT:, :] = jnp.zeros((LRU_HALO_BACK, cg), F32)

    def proj(i, _):
        r0 = pl.multiple_of(i * TOK_TILE, TOK_TILE)
        xn = xn_s[pl.ds(r0, TOK_TILE), :]
        xr_s[pl.ds(r0 + LRU_HALO_FRONT, TOK_TILE), :] = jnp.dot(xn, wxr_ref[...], preferred_element_type=F32)
        gg_s[pl.ds(r0, TOK_TILE), :] = jax.nn.gelu(
            jnp.dot(xn, wgx_ref[...], preferred_element_type=F32))
        return 0
    lax.fori_loop(0, t // TOK_TILE, proj, 0)
    cr_s[LRU_HALO_FRONT:LRU_HALO_FRONT + nctx, :] = jnp.dot(cn_s[...], wxr_ref[...],
                                                            preferred_element_type=F32)

    conv_w = cw_ref[...]
    taps = [conv_w[k:k + 1, :] for k in range(LRU_CONV_W)]
    conv_b = cb_ref[...]
    bcat = bcat_ref[...]
    sp8 = LRU_C * _softplus(-lam_ref[...])
    chunk_ext = LRU_CHUNK + LRU_HALO_FRONT + LRU_HALO_BACK

    def conv(xe):
        before = [xe[SUBLANES * i:SUBLANES * (i + 1)] for i in range(LRU_HALO_FRONT // SUBLANES)]
        after = [xe[LRU_HALO_FRONT + LRU_CHUNK:]]
        return _seg_conv(xe[LRU_HALO_FRONT:LRU_HALO_FRONT + LRU_CHUNK], before, after, taps, conv_b,
                         LRU_CONV_LEFT)
    coeffs = functools.partial(_lru_coeffs, wcat_ref=wcat_ref, bcat=bcat, sp8=sp8)
    n_lat = t // LRU_CHUNK
    n_ctx = nctx // LRU_CHUNK
    ctx_xc = [conv(cr_s[i * LRU_CHUNK:i * LRU_CHUNK + chunk_ext, :]) for i in range(n_ctx)]

    carry = jnp.zeros((1, cg), F32)
    for i in range(n_ctx):
        a, b = coeffs(ctx_xc[i], d=0)
        _, carry = _seg_scan(a, b, carry, reverse=False, want_h=False)

    def fwd(i, carry):
        r0 = pl.multiple_of(i * LRU_CHUNK, LRU_CHUNK)
        xc = conv(xr_s[pl.ds(r0, chunk_ext), :])
        xc_s[pl.ds(r0, LRU_CHUNK), :] = xc
        a, b = coeffs(xc, d=0)
        h, carry = _seg_scan(a, b, carry, reverse=False)
        hf_s[pl.ds(r0, LRU_CHUNK), :] = h
        return carry
    lax.fori_loop(0, n_lat, fwd, carry, unroll=LRU_UNROLL)

    carry = jnp.zeros((1, cg), F32)
    for i in range(n_ctx - 1, -1, -1):
        a, b = coeffs(ctx_xc[i], d=1)
        _, carry = _seg_scan(a, b, carry, reverse=True, want_h=False)

    def bwd(j, carry):
        r0 = pl.multiple_of((n_lat - 1 - j) * LRU_CHUNK, LRU_CHUNK)
        rows = pl.ds(r0, LRU_CHUNK)
        a, b = coeffs(xc_s[rows, :], d=1)
        h, carry = _seg_scan(a, b, carry, reverse=True)
        yp_s[rows, :] = ((hf_s[rows, :] + h) * gg_s[rows, :]).astype(BF16)
        return carry
    lax.fori_loop(0, n_lat, bwd, carry, unroll=LRU_UNROLL)

    def unpermute(i, _):
        r0 = pl.multiple_of(i * TOK_TILE, TOK_TILE)
        rows = pl.ds(r0, TOK_TILE)
        y_ref[rows, :] = _seg_permute_rows(perm, yp_s[rows, :])
        return 0
    lax.fori_loop(0, t // TOK_TILE, unpermute, 0)


def _lru_call(x, ctx, mods3, g_mix, w_in, conv_w, conv_b, wcat, bcat, lam):
    b, t, d = x.shape
    nctx = ctx.shape[1]
    cg = LRU_CG
    ng = D_RNN // cg
    ctx_row = b * N_MOD
    mod = lambda k: pl.BlockSpec((1, 1, D_MODEL), lambda bi, g: (bi * N_MOD + k, 0, 0))
    modc = lambda k: pl.BlockSpec((1, 1, D_MODEL), lambda bi, g: (ctx_row + k, 0, 0))
    in_specs = [
        pl.BlockSpec((None, t, d), lambda bi, g: (bi, 0, 0)),
        pl.BlockSpec((None, nctx, d), lambda bi, g: (bi, 0, 0)),
        mod(0), mod(1), modc(0), modc(1),
        pl.BlockSpec((1, D_MODEL), lambda bi, g: (0, 0)),
        pl.BlockSpec((PERM_ROWS, PERM_ROWS), lambda bi, g: (0, 0)),
        pl.BlockSpec((D_MODEL, cg), lambda bi, g: (0, COL_XR * ng + g)),
        pl.BlockSpec((D_MODEL, cg), lambda bi, g: (0, COL_GX * ng + g)),
        pl.BlockSpec((LRU_CONV_W, cg), lambda bi, g: (0, g)),
        pl.BlockSpec((1, cg), lambda bi, g: (0, g)),
        pl.BlockSpec((2, cg // LRU_BLOCK_W, LRU_BLOCK_W, 2 * LRU_BLOCK_W), lambda bi, g: (0, g, 0, 0)),
        pl.BlockSpec((4, cg), lambda bi, g: (0, g)),
        pl.BlockSpec((2, cg), lambda bi, g: (0, g)),
    ]
    halo = LRU_HALO_FRONT + LRU_HALO_BACK
    scratch = [
        pltpu.VMEM((t, d), BF16), pltpu.VMEM((nctx, d), BF16),
        pltpu.VMEM((t + halo, cg), F32), pltpu.VMEM((nctx + halo, cg), F32),
        pltpu.VMEM((t, cg), F32), pltpu.VMEM((t, cg), F32), pltpu.VMEM((t, cg), F32),
        pltpu.VMEM((t, cg), BF16),
    ]
    bufs = [((t, d), F32, 2), ((nctx, d), F32, 2), ((D_MODEL, cg), BF16, 4), ((t, cg), BF16, 3),
            ((t, d), BF16, 1), ((nctx, d), BF16, 1), ((t + halo, cg), F32, 4),
            ((nctx + halo, cg), F32, 1)]
    return pl.pallas_call(
        _lru_kernel,
        grid=(b, ng),
        in_specs=in_specs,
        out_specs=pl.BlockSpec((None, t, cg), lambda bi, g: (bi, 0, g)),
        out_shape=jax.ShapeDtypeStruct((b, t, D_RNN), BF16),
        scratch_shapes=scratch,
        compiler_params=_params(("arbitrary", "arbitrary"), bufs),
        name="rglru",
    )(x, ctx, mods3, mods3, mods3, mods3, g_mix, _seg_permutation(PERM_ROWS), w_in, w_in,
      conv_w, conv_b, wcat, bcat, lam)


NT_DIMS = (((1,), (1,)), ((), ()))
KEY_ROWS_PER_TILE = LANES // GRID_W


def _attn_group_geometry(g, rows):
    r0 = g * ATTN_GROUP
    starts = [min(max(r0 + o - NA_ROWS // 2, 0), rows - NA_ROWS) for o in range(ATTN_GROUP)]
    key_row0 = starts[0]
    n_key_rows = -(-(starts[-1] + NA_ROWS - key_row0) // KEY_ROWS_PER_TILE) * KEY_ROWS_PER_TILE
    desc = tuple((starts[o] - key_row0, key_row0 - (r0 + o)) for o in range(ATTN_GROUP))
    return key_row0, n_key_rows, desc


BIAS_BOTH, BIAS_RIGHT_ONLY, BIAS_LEFT_ONLY = range(3)


def _bias_tile_index(off, base, tile):
    left = KEY_ROWS_PER_TILE * tile
    in_window = lambda j: off <= j < off + NA_ROWS
    dr_left = base + left + (NA_ROWS - 1)
    if in_window(left) and in_window(left + 1):
        return BIAS_BOTH, dr_left
    if in_window(left + 1):
        return BIAS_RIGHT_ONLY, dr_left + 1
    return BIAS_LEFT_ONLY, dr_left


def _attn_kernel(qr_ref, qp_ref, k_ref, v_ref, kc_ref, vc_ref, bias_ref, y_ref, qs_s, qps_s):
    t = k_ref.shape[0]
    rows = t // GRID_W
    lane = lax.broadcasted_iota(jnp.int32, (rows, GRID_W, LANES), 2)
    head_a = lane < HEAD_DIM

    def stack_heads(q_ref, dst):
        q3 = q_ref[...].reshape(rows, GRID_W, LANES)
        zero = jnp.zeros_like(q3)
        dst[...] = jnp.concatenate([jnp.where(head_a, q3, zero), jnp.where(head_a, zero, q3)], axis=1)
    stack_heads(qr_ref, qs_s)
    stack_heads(qp_ref, qps_s)

    kc = kc_ref[...]
    vc = vc_ref[...]
    out_a = lax.broadcasted_iota(jnp.int32, (GRID_W, LANES), 1) < HEAD_DIM
    rows_stacked = 2 * GRID_W

    def lane_tiles(arrays):
        return [a[:, LANES * i:LANES * (i + 1)] for a in arrays for i in range(a.shape[1] // LANES)]

    for g in range(rows // ATTN_GROUP):
        r0 = g * ATTN_GROUP
        key_row0, n_key_rows, desc = _attn_group_geometry(g, rows)
        k_sup = k_ref[key_row0 * GRID_W:(key_row0 + n_key_rows) * GRID_W, :]
        v_sup = v_ref[key_row0 * GRID_W:(key_row0 + n_key_rows) * GRID_W, :]
        n_tiles = n_key_rows // KEY_ROWS_PER_TILE
        q_grp = qs_s[r0:r0 + ATTN_GROUP].reshape(ATTN_GROUP * rows_stacked, LANES)
        qp_grp = qps_s[r0:r0 + ATTN_GROUP].reshape(ATTN_GROUP * rows_stacked, LANES)
        s_lat = lax.dot_general(q_grp, k_sup, NT_DIMS, preferred_element_type=F32)
        s_ctx = lax.dot_general(qp_grp, kc, NT_DIMS, preferred_element_type=F32)
        p_rows, pc_rows, denoms = [], [], []
        for o in range(ATTN_GROUP):
            off, base = desc[o]
            lo, hi = off // KEY_ROWS_PER_TILE, -(-(off + NA_ROWS) // KEY_ROWS_PER_TILE)
            rsl = slice(o * rows_stacked, (o + 1) * rows_stacked)
            bias = []
            for tile in range(lo, hi):
                kind, dr = _bias_tile_index(off, base, tile)
                bias.append(jnp.concatenate([bias_ref[0, kind, dr], bias_ref[1, kind, dr]], axis=0))
            s_o = s_lat[rsl, lo * LANES:hi * LANES] + jnp.concatenate(bias, axis=-1)
            sc_o = s_ctx[rsl]
            m = jnp.max(functools.reduce(jnp.maximum, lane_tiles([s_o, sc_o])), axis=-1, keepdims=True)
            p_o = jnp.exp(s_o - m)
            pc_o = jnp.exp(sc_o - m)
            denoms.append(jnp.sum(functools.reduce(jnp.add, lane_tiles([p_o, pc_o])), axis=-1, keepdims=True))
            pieces = [p_o.astype(BF16)]
            if lo > 0:
                pieces.insert(0, jnp.zeros((rows_stacked, lo * LANES), BF16))
            if hi < n_tiles:
                pieces.append(jnp.zeros((rows_stacked, (n_tiles - hi) * LANES), BF16))
            p_rows.append(jnp.concatenate(pieces, axis=-1) if len(pieces) > 1 else pieces[0])
            pc_rows.append(pc_o.astype(BF16))
        o_grp = (jnp.dot(jnp.concatenate(p_rows, axis=0), v_sup, preferred_element_type=F32)
                 + jnp.dot(jnp.concatenate(pc_rows, axis=0), vc, preferred_element_type=F32))
        for o in range(ATTN_GROUP):
            blk = o_grp[o * rows_stacked:(o + 1) * rows_stacked] / denoms[o]
            y_ref[(r0 + o) * GRID_W:(r0 + o + 1) * GRID_W, :] = jnp.where(
                out_a, blk[:GRID_W], blk[GRID_W:]).astype(BF16)


def _attn_call(q_rot, q_plain, k_rot, v, kc, vc, bias_tab):
    b, t, _ = q_rot.shape
    nctx = kc.shape[1]
    rows = t // GRID_W
    npair = D_ATT // LANES
    tok = pl.BlockSpec((None, t, LANES), lambda bi, hp: (bi, 0, hp))
    ctx = pl.BlockSpec((None, nctx, LANES), lambda bi, hp: (bi, 0, hp))
    bias_shape = (LANES // HEAD_DIM,) + bias_tab.shape[1:]
    bufs = [((t, LANES), BF16, 10), ((nctx, LANES), BF16, 4), (bias_shape, F32, 2),
            ((rows, 2 * GRID_W, LANES), BF16, 2)]
    return pl.pallas_call(
        _attn_kernel,
        grid=(b, npair),
        in_specs=[tok, tok, tok, tok, ctx, ctx,
                  pl.BlockSpec(bias_shape, lambda bi, hp: (hp, 0, 0, 0, 0))],
        out_specs=tok,
        out_shape=jax.ShapeDtypeStruct((b, t, D_ATT), BF16),
        scratch_shapes=[pltpu.VMEM((rows, 2 * GRID_W, LANES), BF16),
                        pltpu.VMEM((rows, 2 * GRID_W, LANES), BF16)],
        compiler_params=_params(("arbitrary", "arbitrary"), bufs),
        name="nbr_attn",
    )(q_rot, q_plain, k_rot, v, kc, vc, bias_tab)


def _bias_table(rpb):
    c = np.arange(GRID_W)
    col_start = np.clip(c - NA_COLS // 2, 0, GRID_W - NA_COLS)
    in_win = (c[None, :] >= col_start[:, None]) & (c[None, :] < col_start[:, None] + NA_COLS)
    dc = np.clip(c[None, :] - c[:, None], -(NA_COLS - 1), NA_COLS - 1) + (NA_COLS - 1)
    select = (dc[None] == np.arange(2 * NA_COLS - 1)[:, None, None]).astype(np.float32)
    toe = jnp.einsum("hdm,mqk->hdqk", rpb, jnp.asarray(select), precision=lax.Precision.HIGHEST)
    toe = jnp.where(jnp.asarray(in_win), toe, NEG_INF)
    neg = jnp.full_like(toe, NEG_INF)
    nxt = jnp.concatenate([toe[:, 1:], neg[:, :1]], axis=1)
    both = jnp.concatenate([toe, nxt], axis=-1)
    right_only = jnp.concatenate([neg, toe], axis=-1)
    left_only = jnp.concatenate([toe, neg], axis=-1)
    return jnp.stack([both, right_only, left_only], axis=1)


def _merge_kernel(x_ref, yr_ref, yn_ref, sh_ref, sc_ref, gate_ref, g_ref,
                  wgr_ref, wgn_ref, wr_ref, wn_ref, wo_ref, o_ref):
    x = x_ref[...]
    xn = _modulate(x, g_ref[...], sh_ref[0], sc_ref[0]).astype(BF16)
    z_r = jnp.dot(yr_ref[...], wr_ref[...], preferred_element_type=F32)
    z_n = jnp.dot(yn_ref[...], wn_ref[...], preferred_element_type=F32)
    g_r = jax.nn.sigmoid(jnp.dot(xn, wgr_ref[...], preferred_element_type=F32))
    g_n = jax.nn.sigmoid(jnp.dot(xn, wgn_ref[...], preferred_element_type=F32))
    merged = g_r * z_r + g_n * z_n
    o_ref[...] = x + gate_ref[0] * jnp.dot(merged.astype(BF16), wo_ref[...], preferred_element_type=F32)


def _merge_call(x, y_rnn, y_na, mods3, g_mix, w_in, w_rnn_out, w_na_out, w_out):
    b, t, d = x.shape
    tm = TOK_TILE
    tok = pl.BlockSpec((None, tm, d), lambda bi, i: (bi, i, 0))
    mod = lambda k: pl.BlockSpec((1, 1, D_MODEL), lambda bi, i: (bi * N_MOD + k, 0, 0))
    wblk = lambda c: pl.BlockSpec((D_MODEL, D_MODEL), lambda bi, i: (0, c))
    bufs = [((tm, d), F32, 4), ((tm, d), BF16, 4), ((D_MODEL, D_MODEL), BF16, 10), ((tm, d), F32, 4)]
    return pl.pallas_call(
        _merge_kernel,
        grid=(b, t // tm),
        in_specs=[tok, tok, tok, mod(0), mod(1), mod(2),
                  pl.BlockSpec((1, D_MODEL), lambda bi, i: (0, 0)),
                  wblk(COL_MR), wblk(COL_MN), wblk(0), wblk(0), wblk(0)],
        out_specs=tok,
        out_shape=jax.ShapeDtypeStruct((b, t, d), F32),
        compiler_params=_params(("arbitrary", "arbitrary"), bufs),
        name="merge",
    )(x, y_rnn, y_na, mods3, mods3, mods3, g_mix, w_in, w_in, w_rnn_out, w_na_out, w_out)


def _ffn_kernel(x_ref, xp_ref, xnx_ref, sh_ref, sc_ref, gate_ref, g_ref, wup_ref, cw_ref, cb_ref,
                wdn_ref, o_ref, act_s):
    tm = x_ref.shape[0]
    i = pl.program_id(1)
    x = x_ref[...]
    xe = jnp.concatenate([xp_ref[...], x, xnx_ref[...]], axis=0)
    xn = _modulate(xe, g_ref[...], sh_ref[0], sc_ref[0])
    row = lax.broadcasted_iota(jnp.int32, (tm + 2 * HALO, 1), 0)
    outside = ((row < HALO) & (i == 0)) | ((row >= tm + HALO) & (i == pl.num_programs(1) - 1))
    xn = jnp.where(outside, 0.0, xn).astype(BF16)
    cw = cw_ref[...]
    cb = cb_ref[...]
    ts = tm // FFN_SUBTILES
    for s in range(FFN_SUBTILES):
        xs = xn[s * ts:s * ts + ts + 2 * HALO]
        for c in range(D_FF // FF_CHUNK):
            halves = []
            for half in range(2):
                sl = slice(half * D_FF + c * FF_CHUNK, half * D_FF + (c + 1) * FF_CHUNK)
                u = jnp.dot(xs, wup_ref[:, sl], preferred_element_type=F32)
                acc = cb[:, sl]
                for k in range(FFN_CONV_W):
                    off = HALO - FFN_CONV_LEFT + k
                    acc = acc + cw[k:k + 1, sl] * u[off:off + ts]
                halves.append(acc)
            a, g = halves
            act_s[s, :, c * FF_CHUNK:(c + 1) * FF_CHUNK] = ((a * jax.nn.sigmoid(a)) * g).astype(BF16)
        rows = slice(s * ts, (s + 1) * ts)
        o_ref[rows, :] = x[rows] + gate_ref[0] * jnp.dot(act_s[s], wdn_ref[...], preferred_element_type=F32)


def _ffn_call(x1, mods3, g_ffn, w_up, conv_w, conv_b, w_down):
    b, t, d = x1.shape
    tm = TOK_TILE
    hb = tm // HALO
    n_halo_blocks = t // HALO
    tok = pl.BlockSpec((None, tm, d), lambda bi, i: (bi, i, 0))
    prev = pl.BlockSpec((None, HALO, d), lambda bi, i: (bi, jnp.maximum(i * hb - 1, 0), 0))
    nxt = pl.BlockSpec((None, HALO, d), lambda bi, i: (bi, jnp.minimum((i + 1) * hb, n_halo_blocks - 1), 0))
    mod = lambda k: pl.BlockSpec((1, 1, D_MODEL), lambda bi, i: (bi * N_MOD + k, 0, 0))
    const = lambda shape: pl.BlockSpec(shape, lambda bi, i: (0,) * len(shape),
                                       pipeline_mode=pl.Buffered(1))
    bufs = [((tm, d), F32, 4), ((D_MODEL, 2 * D_FF), BF16, 1), ((D_FF, D_MODEL), BF16, 1),
            ((tm, D_FF), BF16, 1), ((tm + 2 * HALO, d), F32, 2)]
    return pl.pallas_call(
        _ffn_kernel,
        grid=(b, t // tm),
        in_specs=[tok, prev, nxt, mod(3), mod(4), mod(5), const((1, D_MODEL)),
                  const((D_MODEL, 2 * D_FF)), const((FFN_CONV_W, 2 * D_FF)), const((1, 2 * D_FF)),
                  const((D_FF, D_MODEL))],
        out_specs=tok,
        out_shape=jax.ShapeDtypeStruct((b, t, d), F32),
        scratch_shapes=[pltpu.VMEM((FFN_SUBTILES, tm // FFN_SUBTILES, D_FF), BF16)],
        compiler_params=_params(("arbitrary", "arbitrary"), bufs),
        name="conv_ffn",
    )(x1, x1, x1, mods3, mods3, mods3, g_ffn, w_up, conv_w, conv_b, w_down)


def _rope_tables(t):
    n = HEAD_DIM // 4
    freq = (ROPE_BASE ** (-np.arange(n, dtype=np.float32) / n)).astype(np.float32)
    pos = np.arange(t)
    ang_r = (pos // GRID_W).astype(np.float32)[:, None] * freq
    ang_c = (pos % GRID_W).astype(np.float32)[:, None] * freq
    cos = np.concatenate([np.cos(ang_r)] * 2 + [np.cos(ang_c)] * 2, axis=-1)
    sin = np.concatenate([-np.sin(ang_r), np.sin(ang_r), -np.sin(ang_c), np.sin(ang_c)], axis=-1)
    reps = LANES // HEAD_DIM
    return (jnp.asarray(np.tile(cos, (1, reps)), F32), jnp.asarray(np.tile(sin, (1, reps)), F32))


def kernel(x, c, ctx, c_ctx, w_mod, b_mod, norm_mix_g, norm_ffn_g, w_in, lru_conv_w, lru_conv_b, lru_wa, lru_ba, lru_wx, lru_bx, lru_lambda, q_norm_g, k_norm_g, na_rpb, w_rnn_out, w_na_out, w_out, w_up, ffn_conv_w, ffn_conv_b, w_down):
    depth = w_mod.shape[0]
    assert depth == 1, "single-layer block"
    b, t, d = x.shape
    assert d == D_MODEL and b < MOD_ROWS and t % TOK_TILE == 0 and t % GRID_W == 0
    l = 0

    cc = jnp.zeros((MOD_ROWS, d), F32).at[:b].set(c).at[b].set(c_ctx)
    w_in_b = w_in[l].astype(BF16)
    g_mix = norm_mix_g[l][None, :]
    g_ffn = norm_ffn_g[l][None, :]
    q_gain = jnp.tile(q_norm_g[l], NA_HEADS)[None, :]
    k_gain = jnp.tile(k_norm_g[l], NA_HEADS)[None, :]
    head_of = np.arange(MXU_DIM) // HEAD_DIM
    e_mat = jnp.asarray(np.where(head_of[:, None] == head_of[None, :], 1.0 / HEAD_DIM, 0.0), BF16)
    cos_t, sin_t = _rope_tables(t)
    wcat = jnp.concatenate([lru_wa[l], lru_wx[l]], axis=-1).astype(BF16)
    bcat = jnp.stack([lru_ba[l, 0], lru_bx[l, 0], lru_ba[l, 1], lru_bx[l, 1]])
    bias_tab = _bias_table(na_rpb[l])

    mods = _mod_call(cc, w_mod[l], b_mod[l][None, :])
    mods3 = mods.reshape(MOD_ROWS * N_MOD, 1, d)

    k_rot, v, q_rot, q_plain = _qkv_call(x, mods3, 0, g_mix, w_in_b, k_gain, q_gain, e_mat,
                                         cos_t, sin_t, latent=True)
    kc, vc = _qkv_call(ctx, mods3, b, g_mix, w_in_b, k_gain, None, e_mat, None, None, latent=False)
    y_rnn = _lru_call(x, ctx, mods3, g_mix, w_in_b, lru_conv_w[l], lru_conv_b[l][None, :],
                      wcat, bcat, lru_lambda[l])
    y_na = _attn_call(q_rot, q_plain, k_rot, v, kc, vc, bias_tab)
    x1 = _merge_call(x, y_rnn, y_na, mods3, g_mix, w_in_b, w_rnn_out[l].astype(BF16),
                     w_na_out[l].astype(BF16), w_out[l].astype(BF16))
    return _ffn_call(x1, mods3, g_ffn, w_up[l].astype(BF16), ffn_conv_w[l], ffn_conv_b[l][None, :],
                     w_down[l].astype(BF16))
```

```python
import functools

import jax
import jax.numpy as jnp
import numpy as np
from jax import lax
from jax.experimental import pallas as pl
from jax.experimental.pallas import tpu as pltpu

F32 = jnp.float32
BF16 = jnp.bfloat16

D_MODEL = 1024
GRID_W = 64
D_RNN = D_MODEL
LRU_BLOCKS = 8
LRU_BLOCK_W = D_RNN // LRU_BLOCKS
LRU_CONV_W = 4
LRU_CONV_LEFT = 2
LRU_C = 8.0
NA_HEADS = 16
HEAD_DIM = 64
D_ATT = NA_HEADS * HEAD_DIM
NA_ROWS = 8
NA_COLS = 16
ROPE_BASE = 10000.0
D_FF = ((8 * D_MODEL // 3 + 127) // 128) * 128
FFN_CONV_W = 3
FFN_CONV_LEFT = 1
N_MOD = 6
EPS = 1e-6
NEG_INF = -1e30

COL_XR, COL_K, COL_V, COL_GX, COL_Q, COL_MR, COL_MN = range(7)

LANES = 128
SUBLANES = 8
MXU_DIM = 256
VMEM_LIMIT_CAP = 60000 * 1024
VMEM_TEMP_HEADROOM = 12 * 1024 * 1024

MOD_ROWS = 16
TOK_TILE = 512
LRU_CG = 256
LRU_CHUNK = 128
HALO = SUBLANES
FF_CHUNK = MXU_DIM
FFN_SUBTILES = 2
QKV_SUBTILES = 2
PERM_ROWS = 128
ATTN_GROUP = 4


def _nbytes(shape, dtype):
    n = 1
    for s in shape:
        n *= s
    return n * jnp.dtype(dtype).itemsize


def _vmem_limit(buffers):
    need = sum(_nbytes(s, d) * c for s, d, c in buffers) + VMEM_TEMP_HEADROOM
    return int(min(need, VMEM_LIMIT_CAP))


def _params(semantics, buffers):
    return pltpu.CompilerParams(dimension_semantics=semantics,
                                vmem_limit_bytes=_vmem_limit(buffers))


def _modulate(x, g, shift, scale):
    ms = jnp.mean(x * x, axis=-1, keepdims=True)
    y = x * lax.rsqrt(ms + EPS)
    return (y * g) * (1.0 + scale) + shift


def _mod_kernel(cc_ref, w_ref, b_ref, o_ref):
    s = cc_ref[...]
    s = s * jax.nn.sigmoid(s)
    o_ref[...] = jnp.dot(s.astype(BF16), w_ref[...].astype(BF16),
                         preferred_element_type=F32) + b_ref[...]


def _mod_call(cc, w_mod, b_mod):
    n = w_mod.shape[1]
    tn = D_MODEL
    bufs = [((MOD_ROWS, D_MODEL), F32, 2), ((D_MODEL, tn), F32, 2), ((MOD_ROWS, tn), F32, 2)]
    return pl.pallas_call(
        _mod_kernel,
        grid=(n // tn,),
        in_specs=[pl.BlockSpec((MOD_ROWS, D_MODEL), lambda j: (0, 0)),
                  pl.BlockSpec((D_MODEL, tn), lambda j: (0, j)),
                  pl.BlockSpec((1, tn), lambda j: (0, j))],
        out_specs=pl.BlockSpec((MOD_ROWS, tn), lambda j: (0, j)),
        out_shape=jax.ShapeDtypeStruct((MOD_ROWS, n), F32),
        compiler_params=_params(("arbitrary",), bufs),
        name="adaln_mod",
    )(cc, w_mod, b_mod)


def _head_rms(t, gain, e):
    sq = t * t
    hi = sq.astype(BF16)
    lo = (sq - hi.astype(F32)).astype(BF16)
    ms = jnp.dot(hi, e, preferred_element_type=F32) + jnp.dot(lo, e, preferred_element_type=F32)
    return (t * lax.rsqrt(ms + EPS)) * gain


def _rope(y, cos, sin_signed, first_half):
    partner = jnp.where(first_half, pltpu.roll(y, LANES - 16, 1), pltpu.roll(y, 16, 1))
    return y * cos + partner * sin_signed


def _qkv_kernel(*refs, latent):
    if latent:
        (x_ref, sh_ref, sc_ref, g_ref, wk_ref, wv_ref, wq_ref, kg_ref, qg_ref, e_ref,
         cos_ref, sin_ref, k_out, v_out, qr_out, qp_out) = refs
    else:
        (x_ref, sh_ref, sc_ref, g_ref, wk_ref, wv_ref, kg_ref, e_ref, k_out, v_out) = refs
    e = e_ref[...]
    scale = HEAD_DIM ** -0.5
    tm = x_ref.shape[0]
    ts = tm // QKV_SUBTILES
    for s in range(QKV_SUBTILES):
        rows = slice(s * ts, (s + 1) * ts)
        xn = _modulate(x_ref[rows, :], g_ref[...], sh_ref[0], sc_ref[0]).astype(BF16)
        if latent:
            cos = cos_ref[rows, :]
            sin = sin_ref[rows, :]
            lane = lax.broadcasted_iota(jnp.int32, cos.shape, 1)
            first_half = (lane % 32) < 16
        for j in range(D_ATT // MXU_DIM):
            sl = slice(MXU_DIM * j, MXU_DIM * (j + 1))
            v_out[rows, sl] = jnp.dot(xn, wv_ref[:, sl], preferred_element_type=F32).astype(BF16)
            k = _head_rms(jnp.dot(xn, wk_ref[:, sl], preferred_element_type=F32), kg_ref[:, sl], e)
            if not latent:
                k_out[rows, sl] = k.astype(BF16)
                continue
            q = _head_rms(jnp.dot(xn, wq_ref[:, sl], preferred_element_type=F32), qg_ref[:, sl], e)
            qp_out[rows, sl] = (q * scale).astype(BF16)
            for h in range(MXU_DIM // LANES):
                hl = slice(LANES * h, LANES * (h + 1))
                ol = slice(MXU_DIM * j + LANES * h, MXU_DIM * j + LANES * (h + 1))
                k_out[rows, ol] = _rope(k[:, hl], cos, sin, first_half).astype(BF16)
                qr_out[rows, ol] = (_rope(q[:, hl], cos, sin, first_half) * scale).astype(BF16)


def _qkv_call(x, mods3, mod_row0, g_mix, w_in, k_gain, q_gain, e_mat, cos_t, sin_t, latent):
    b, t, d = x.shape
    tm = min(TOK_TILE, t)
    nt = t // tm
    wblk = lambda c: pl.BlockSpec((D_MODEL, D_MODEL), lambda bi, i: (0, c))
    vec = pl.BlockSpec((1, D_MODEL), lambda bi, i: (0, 0))
    if latent:
        mod = lambda k: pl.BlockSpec((1, 1, D_MODEL), lambda bi, i: (bi * N_MOD + k, 0, 0))
    else:
        mod = lambda k: pl.BlockSpec((1, 1, D_MODEL), lambda bi, i: (mod_row0 * N_MOD + k, 0, 0))
    tok = pl.BlockSpec((None, tm, d), lambda bi, i: (bi, i, 0))
    in_specs = [tok, mod(0), mod(1), vec, wblk(COL_K), wblk(COL_V)]
    args = [x, mods3, mods3, g_mix, w_in, w_in]
    if latent:
        in_specs += [wblk(COL_Q), vec, vec]
        args += [w_in, k_gain, q_gain]
    else:
        in_specs += [vec]
        args += [k_gain]
    in_specs += [pl.BlockSpec((MXU_DIM, MXU_DIM), lambda bi, i: (0, 0))]
    args += [e_mat]
    n_out = 2
    if latent:
        rope_spec = pl.BlockSpec((tm, LANES), lambda bi, i: (i, 0))
        in_specs += [rope_spec, rope_spec]
        args += [cos_t, sin_t]
        n_out = 4
    out_sd = jax.ShapeDtypeStruct((b, t, D_ATT), BF16)
    bufs = [((tm, d), F32, 2), ((D_MODEL, D_MODEL), BF16, 6), ((tm, D_ATT), BF16, 2 * n_out),
            ((tm, LANES), F32, 4), ((tm, d), BF16, 1)]
    return pl.pallas_call(
        functools.partial(_qkv_kernel, latent=latent),
        grid=(b, nt),
        in_specs=in_specs,
        out_specs=[tok] * n_out,
        out_shape=[out_sd] * n_out,
        compiler_params=_params(("arbitrary", "arbitrary"), bufs),
        name="qkv_proj" if latent else "ctx_proj",
    )(*args)


def _softplus(x):
    return jnp.maximum(x, 0.0) + jnp.log1p(jnp.exp(-jnp.abs(x)))


SEG_BLOCK = SUBLANES * SUBLANES
LRU_HALO_FRONT = 2 * SUBLANES
LRU_HALO_BACK = SUBLANES


def _seg_permutation(n):
    p = np.arange(n)
    src = (p // SEG_BLOCK) * SEG_BLOCK + (p % SUBLANES) * SUBLANES + (p % SEG_BLOCK) // SUBLANES
    return jnp.asarray(src[:, None] == p[None, :], BF16)


def _seg_permute_rows(perm, x):
    n = perm.shape[0]
    parts = [jnp.dot(perm, x[n * i:n * (i + 1)], preferred_element_type=F32) for i in range(x.shape[0] // n)]
    return jnp.concatenate(parts, axis=0).astype(BF16)


def _seg_conv(x, before, after, taps, bias, left):
    n, c = x.shape
    right = len(taps) - 1 - left
    nb = n // SEG_BLOCK
    tiles = [x[SUBLANES * i:SUBLANES * (i + 1)] for i in range(n // SUBLANES)]
    sub = lax.broadcasted_iota(jnp.int32, (SUBLANES, c), 0)
    out = []
    for blk in range(nb):
        cur = tiles[SUBLANES * blk:SUBLANES * (blk + 1)]
        down, up = {}, {}
        for e in range(1, left + 1):
            j = SUBLANES - e
            prev = tiles[SUBLANES * (blk - 1) + j] if blk > 0 else before[left - e]
            down[j] = jnp.where(sub == 0, pltpu.roll(prev, 1, 0), pltpu.roll(cur[j], 1, 0))
        for e in range(right):
            nxt = tiles[SUBLANES * (blk + 1) + e] if blk + 1 < nb else after[e]
            up[e] = jnp.where(sub == SUBLANES - 1, pltpu.roll(nxt, SUBLANES - 1, 0),
                              pltpu.roll(cur[e], SUBLANES - 1, 0))
        for j in range(SUBLANES):
            acc = bias
            for k, wk in enumerate(taps):
                src = j + k - left
                if src < 0:
                    tile = down[src + SUBLANES]
                elif src >= SUBLANES:
                    tile = up[src - SUBLANES]
                else:
                    tile = cur[src]
                acc = acc + wk * tile
            out.append(acc)
    return jnp.concatenate(out, axis=0)


def _lru_coeffs(xc, wcat_ref, bcat, sp8, d):
    xcb = xc.astype(BF16)
    rs, gs = [], []
    for blk in range(LRU_CG // LRU_BLOCK_W):
        bl = slice(LRU_BLOCK_W * blk, LRU_BLOCK_W * (blk + 1))
        z = jnp.dot(xcb[:, bl], wcat_ref[d, blk], preferred_element_type=F32)
        rs.append(z[:, :LRU_BLOCK_W])
        gs.append(z[:, LRU_BLOCK_W:])
    r = jax.nn.sigmoid(jnp.concatenate(rs, axis=-1) + bcat[2 * d:2 * d + 1, :])
    gate = jax.nn.sigmoid(jnp.concatenate(gs, axis=-1) + bcat[2 * d + 1:2 * d + 2, :])
    a = jnp.exp(r * -sp8[d:d + 1, :])
    w = jnp.tanh(r * sp8[d:d + 1, :]) * (a * a + 1.0)
    root = jnp.where(w > 0.0, w * lax.rsqrt(w), 0.0)
    b = root * gate * xc
    return a, b


def _seg_scan(a, b, carry, reverse, want_h=True):
    n, c = a.shape
    nb = n // SEG_BLOCK
    sub = lax.broadcasted_iota(jnp.int32, (SUBLANES, c), 0)
    steps = range(SUBLANES - 1, -1, -1) if reverse else range(SUBLANES)
    hs = [None] * (nb * SUBLANES)
    for blk in (range(nb - 1, -1, -1) if reverse else range(nb)):
        tile = lambda x, j: x[blk * SEG_BLOCK + j * SUBLANES:blk * SEG_BLOCK + (j + 1) * SUBLANES]
        acc_a, acc_h = {}, {}
        prev = None
        for j in steps:
            aj, bj = tile(a, j), tile(b, j)
            acc_h[j] = bj if prev is None else aj * acc_h[prev] + bj
            acc_a[j] = aj if prev is None else aj * acc_a[prev]
            prev = j
        pa, ph = acc_a[prev], acc_h[prev]
        for s in (1, 2, 4):
            shift = SUBLANES - s if reverse else s
            m = (sub < SUBLANES - s) if reverse else (sub >= s)
            ph = jnp.where(m, pa * pltpu.roll(ph, shift, 0) + ph, ph)
            pa = jnp.where(m, pa * pltpu.roll(pa, shift, 0), pa)
        end_state = pa * carry + ph
        if reverse:
            seg_in = jnp.where(sub == SUBLANES - 1, carry, pltpu.roll(end_state, SUBLANES - 1, 0))
            carry = end_state[0:1]
        else:
            seg_in = jnp.where(sub == 0, carry, pltpu.roll(end_state, 1, 0))
            carry = end_state[SUBLANES - 1:SUBLANES]
        if want_h:
            for j in steps:
                hs[blk * SUBLANES + j] = acc_h[j] + acc_a[j] * seg_in
    return (jnp.concatenate(hs, axis=0) if want_h else None), carry


def _lru_kernel(x_ref, ctx_ref, shx_ref, scx_ref, shc_ref, scc_ref, g_ref, perm_ref, wxr_ref, wgx_ref,
                cw_ref, cb_ref, wcat_ref, bcat_ref, lam_ref, y_ref,
                xn_s, cn_s, xr_s, cr_s, xc_s, hf_s):
    t = x_ref.shape[0]
    nctx = ctx_ref.shape[0]
    cg = y_ref.shape[1]
    perm = perm_ref[...]

    def permuted_norm(rows_f32, shift, scale):
        return _seg_permute_rows(perm, _modulate(rows_f32, g_ref[...], shift, scale).astype(BF16))

    @pl.when(pl.program_id(1) == 0)
    def _():
        def body(i, _):
            r0 = pl.multiple_of(i * TOK_TILE, TOK_TILE)
            xn_s[pl.ds(r0, TOK_TILE), :] = permuted_norm(x_ref[pl.ds(r0, TOK_TILE), :], shx_ref[0], scx_ref[0])
            return 0
        lax.fori_loop(0, t // TOK_TILE, body, 0)
        cn_s[...] = permuted_norm(ctx_ref[...], shc_ref[0], scc_ref[0])

    xr_s[0:LRU_HALO_FRONT, :] = jnp.zeros((LRU_HALO_FRONT, cg), F32)
    xr_s[t + LRU_HALO_FRONT:, :] = jnp.zeros((LRU_HALO_BACK, cg), F32)
    cr_s[0:LRU_HALO_FRONT, :] = jnp.zeros((LRU_HALO_FRONT, cg), F32)
    cr_s[nctx + LRU_HALO_FRONT:, :] = jnp.zeros((LRU_HALO_BACK, cg), F32)

    def project_xr(blk):
        rows = slice(blk * TOK_TILE, (blk + 1) * TOK_TILE)
        xr_s[LRU_HALO_FRONT + blk * TOK_TILE:LRU_HALO_FRONT + (blk + 1) * TOK_TILE, :] = jnp.dot(
            xn_s[rows, :], wxr_ref[...], preferred_element_type=F32)
    cr_s[LRU_HALO_FRONT:LRU_HALO_FRONT + nctx, :] = jnp.dot(cn_s[...], wxr_ref[...],
                                                            preferred_element_type=F32)
    project_xr(0)

    conv_w = cw_ref[...]
    taps = [conv_w[k:k + 1, :] for k in range(LRU_CONV_W)]
    conv_b = cb_ref[...]
    bcat = bcat_ref[...]
    sp8 = LRU_C * _softplus(-lam_ref[...])
    chunk_ext = LRU_CHUNK + LRU_HALO_FRONT + LRU_HALO_BACK

    def conv(xe):
        before = [xe[SUBLANES * i:SUBLANES * (i + 1)] for i in range(LRU_HALO_FRONT // SUBLANES)]
        after = [xe[LRU_HALO_FRONT + LRU_CHUNK:]]
        return _seg_conv(xe[LRU_HALO_FRONT:LRU_HALO_FRONT + LRU_CHUNK], before, after, taps, conv_b,
                         LRU_CONV_LEFT)
    coeffs = functools.partial(_lru_coeffs, wcat_ref=wcat_ref, bcat=bcat, sp8=sp8)
    n_ctx = nctx // LRU_CHUNK
    ctx_xc = [conv(cr_s[i * LRU_CHUNK:i * LRU_CHUNK + chunk_ext, :]) for i in range(n_ctx)]

    carry = jnp.zeros((1, cg), F32)
    for i in range(n_ctx):
        a, b = coeffs(ctx_xc[i], d=0)
        _, carry = _seg_scan(a, b, carry, reverse=False, want_h=False)

    chunks_per_blk = TOK_TILE // LRU_CHUNK
    n_blk = t // TOK_TILE
    for blk in range(n_blk):
        if blk + 1 < n_blk:
            project_xr(blk + 1)
        for c in range(chunks_per_blk):
            r0 = (blk * chunks_per_blk + c) * LRU_CHUNK
            xc = conv(xr_s[r0:r0 + chunk_ext, :])
            xc_s[r0:r0 + LRU_CHUNK, :] = xc
            a, b = coeffs(xc, d=0)
            h, carry = _seg_scan(a, b, carry, reverse=False)
            hf_s[r0:r0 + LRU_CHUNK, :] = h

    carry = jnp.zeros((1, cg), F32)
    for i in range(n_ctx - 1, -1, -1):
        a, b = coeffs(ctx_xc[i], d=1)
        _, carry = _seg_scan(a, b, carry, reverse=True, want_h=False)

    def store_unpermuted(blk, y_perm):
        y_ref[blk * TOK_TILE:(blk + 1) * TOK_TILE, :] = _seg_permute_rows(perm, y_perm)

    pending = None
    for blk in range(n_blk - 1, -1, -1):
        gate = jax.nn.gelu(jnp.dot(xn_s[blk * TOK_TILE:(blk + 1) * TOK_TILE, :], wgx_ref[...],
                                   preferred_element_type=F32))
        ys = [None] * chunks_per_blk
        for c in range(chunks_per_blk - 1, -1, -1):
            r0 = (blk * chunks_per_blk + c) * LRU_CHUNK
            a, b = coeffs(xc_s[r0:r0 + LRU_CHUNK, :], d=1)
            h, carry = _seg_scan(a, b, carry, reverse=True)
            ys[c] = ((hf_s[r0:r0 + LRU_CHUNK, :] + h)
                     * gate[c * LRU_CHUNK:(c + 1) * LRU_CHUNK]).astype(BF16)
        if pending is not None:
            store_unpermuted(*pending)
        pending = (blk, jnp.concatenate(ys, axis=0))
    store_unpermuted(*pending)


def _lru_call(x, ctx, mods3, g_mix, w_in, conv_w, conv_b, wcat, bcat, lam):
    b, t, d = x.shape
    nctx = ctx.shape[1]
    cg = LRU_CG
    ng = D_RNN // cg
    ctx_row = b * N_MOD
    mod = lambda k: pl.BlockSpec((1, 1, D_MODEL), lambda bi, g: (bi * N_MOD + k, 0, 0))
    modc = lambda k: pl.BlockSpec((1, 1, D_MODEL), lambda bi, g: (ctx_row + k, 0, 0))
    in_specs = [
        pl.BlockSpec((None, t, d), lambda bi, g: (bi, 0, 0)),
        pl.BlockSpec((None, nctx, d), lambda bi, g: (bi, 0, 0)),
        mod(0), mod(1), modc(0), modc(1),
        pl.BlockSpec((1, D_MODEL), lambda bi, g: (0, 0)),
        pl.BlockSpec((PERM_ROWS, PERM_ROWS), lambda bi, g: (0, 0)),
        pl.BlockSpec((D_MODEL, cg), lambda bi, g: (0, COL_XR * ng + g)),
        pl.BlockSpec((D_MODEL, cg), lambda bi, g: (0, COL_GX * ng + g)),
        pl.BlockSpec((LRU_CONV_W, cg), lambda bi, g: (0, g)),
        pl.BlockSpec((1, cg), lambda bi, g: (0, g)),
        pl.BlockSpec((2, cg // LRU_BLOCK_W, LRU_BLOCK_W, 2 * LRU_BLOCK_W), lambda bi, g: (0, g, 0, 0)),
        pl.BlockSpec((4, cg), lambda bi, g: (0, g)),
        pl.BlockSpec((2, cg), lambda bi, g: (0, g)),
    ]
    halo = LRU_HALO_FRONT + LRU_HALO_BACK
    scratch = [
        pltpu.VMEM((t, d), BF16), pltpu.VMEM((nctx, d), BF16),
        pltpu.VMEM((t + halo, cg), F32), pltpu.VMEM((nctx + halo, cg), F32),
        pltpu.VMEM((t, cg), F32), pltpu.VMEM((t, cg), F32),
    ]
    bufs = [((t, d), F32, 2), ((nctx, d), F32, 2), ((D_MODEL, cg), BF16, 4), ((t, cg), BF16, 2),
            ((t, d), BF16, 1), ((nctx, d), BF16, 1), ((t + halo, cg), F32, 3),
            ((nctx + halo, cg), F32, 1)]
    return pl.pallas_call(
        _lru_kernel,
        grid=(b, ng),
        in_specs=in_specs,
        out_specs=pl.BlockSpec((None, t, cg), lambda bi, g: (bi, 0, g)),
        out_shape=jax.ShapeDtypeStruct((b, t, D_RNN), BF16),
        scratch_shapes=scratch,
        compiler_params=_params(("arbitrary", "arbitrary"), bufs),
        name="rglru",
    )(x, ctx, mods3, mods3, mods3, mods3, g_mix, _seg_permutation(PERM_ROWS), w_in, w_in,
      conv_w, conv_b, wcat, bcat, lam)


NT_DIMS = (((1,), (1,)), ((), ()))
KEY_ROWS_PER_TILE = LANES // GRID_W


def _attn_group_geometry(g, rows):
    r0 = g * ATTN_GROUP
    starts = [min(max(r0 + o - NA_ROWS // 2, 0), rows - NA_ROWS) for o in range(ATTN_GROUP)]
    key_row0 = starts[0]
    n_key_rows = -(-(starts[-1] + NA_ROWS - key_row0) // KEY_ROWS_PER_TILE) * KEY_ROWS_PER_TILE
    desc = tuple((starts[o] - key_row0, key_row0 - (r0 + o)) for o in range(ATTN_GROUP))
    return key_row0, n_key_rows, desc


BIAS_BOTH, BIAS_RIGHT_ONLY, BIAS_LEFT_ONLY = range(3)


def _bias_tile_index(off, base, tile):
    left = KEY_ROWS_PER_TILE * tile
    in_window = lambda j: off <= j < off + NA_ROWS
    dr_left = base + left + (NA_ROWS - 1)
    if in_window(left) and in_window(left + 1):
        return BIAS_BOTH, dr_left
    if in_window(left + 1):
        return BIAS_RIGHT_ONLY, dr_left + 1
    return BIAS_LEFT_ONLY, dr_left


def _attn_kernel(qr_ref, qp_ref, k_ref, v_ref, kc_ref, vc_ref, bias_ref, y_ref, qs_s, qps_s):
    t = k_ref.shape[0]
    rows = t // GRID_W
    lane = lax.broadcasted_iota(jnp.int32, (rows, GRID_W, LANES), 2)
    head_a = lane < HEAD_DIM

    def stack_heads(q_ref, dst):
        q3 = q_ref[...].reshape(rows, GRID_W, LANES)
        zero = jnp.zeros_like(q3)
        dst[...] = jnp.concatenate([jnp.where(head_a, q3, zero), jnp.where(head_a, zero, q3)], axis=1)
    stack_heads(qr_ref, qs_s)
    stack_heads(qp_ref, qps_s)

    kc = kc_ref[...]
    vc = vc_ref[...]
    out_a = lax.broadcasted_iota(jnp.int32, (GRID_W, LANES), 1) < HEAD_DIM
    rows_stacked = 2 * GRID_W

    def lane_tiles(arrays):
        return [a[:, LANES * i:LANES * (i + 1)] for a in arrays for i in range(a.shape[1] // LANES)]

    for g in range(rows // ATTN_GROUP):
        r0 = g * ATTN_GROUP
        key_row0, n_key_rows, desc = _attn_group_geometry(g, rows)
        k_sup = k_ref[key_row0 * GRID_W:(key_row0 + n_key_rows) * GRID_W, :]
        v_sup = v_ref[key_row0 * GRID_W:(key_row0 + n_key_rows) * GRID_W, :]
        n_tiles = n_key_rows // KEY_ROWS_PER_TILE
        q_grp = qs_s[r0:r0 + ATTN_GROUP].reshape(ATTN_GROUP * rows_stacked, LANES)
        qp_grp = qps_s[r0:r0 + ATTN_GROUP].reshape(ATTN_GROUP * rows_stacked, LANES)
        s_lat = lax.dot_general(q_grp, k_sup, NT_DIMS, preferred_element_type=F32)
        s_ctx = lax.dot_general(qp_grp, kc, NT_DIMS, preferred_element_type=F32)
        p_rows, pc_rows, denoms = [], [], []
        for o in range(ATTN_GROUP):
            off, base = desc[o]
            lo, hi = off // KEY_ROWS_PER_TILE, -(-(off + NA_ROWS) // KEY_ROWS_PER_TILE)
            rsl = slice(o * rows_stacked, (o + 1) * rows_stacked)
            bias = []
            for tile in range(lo, hi):
                kind, dr = _bias_tile_index(off, base, tile)
                halves = []
                for head in range(LANES // HEAD_DIM):
                    left = bias_ref[head, dr] if kind != BIAS_RIGHT_ONLY else NEG_INF
                    right = (bias_ref[head, dr + 1] if kind == BIAS_BOTH else
                             bias_ref[head, dr] if kind == BIAS_RIGHT_ONLY else NEG_INF)
                    halves.append(jnp.where(out_a, left, right))
                bias.append(jnp.concatenate(halves, axis=0))
            s_o = s_lat[rsl, lo * LANES:hi * LANES] + jnp.concatenate(bias, axis=-1)
            sc_o = s_ctx[rsl]
            m = jnp.max(functools.reduce(jnp.maximum, lane_tiles([s_o, sc_o])), axis=-1, keepdims=True)
            p_o = jnp.exp(s_o - m)
            pc_o = jnp.exp(sc_o - m)
            denoms.append(jnp.sum(functools.reduce(jnp.add, lane_tiles([p_o, pc_o])), axis=-1, keepdims=True))
            pieces = [p_o.astype(BF16)]
            if lo > 0:
                pieces.insert(0, jnp.zeros((rows_stacked, lo * LANES), BF16))
            if hi < n_tiles:
                pieces.append(jnp.zeros((rows_stacked, (n_tiles - hi) * LANES), BF16))
            p_rows.append(jnp.concatenate(pieces, axis=-1) if len(pieces) > 1 else pieces[0])
            pc_rows.append(pc_o.astype(BF16))
        o_grp = (jnp.dot(jnp.concatenate(p_rows, axis=0), v_sup, preferred_element_type=F32)
                 + jnp.dot(jnp.concatenate(pc_rows, axis=0), vc, preferred_element_type=F32))
        for o in range(ATTN_GROUP):
            blk = o_grp[o * rows_stacked:(o + 1) * rows_stacked] / denoms[o]
            y_ref[(r0 + o) * GRID_W:(r0 + o + 1) * GRID_W, :] = jnp.where(
                out_a, blk[:GRID_W], blk[GRID_W:]).astype(BF16)


def _attn_call(q_rot, q_plain, k_rot, v, kc, vc, bias_tab):
    b, t, _ = q_rot.shape
    nctx = kc.shape[1]
    rows = t // GRID_W
    npair = D_ATT // LANES
    tok = pl.BlockSpec((None, t, LANES), lambda bi, hp: (bi, 0, hp))
    ctx = pl.BlockSpec((None, nctx, LANES), lambda bi, hp: (bi, 0, hp))
    bias_shape = (LANES // HEAD_DIM,) + bias_tab.shape[1:]
    bufs = [((t, LANES), BF16, 10), ((nctx, LANES), BF16, 4), (bias_shape, F32, 2),
            ((rows, 2 * GRID_W, LANES), BF16, 2)]
    return pl.pallas_call(
        _attn_kernel,
        grid=(b, npair),
        in_specs=[tok, tok, tok, tok, ctx, ctx,
                  pl.BlockSpec(bias_shape, lambda bi, hp: (hp, 0, 0, 0))],
        out_specs=tok,
        out_shape=jax.ShapeDtypeStruct((b, t, D_ATT), BF16),
        scratch_shapes=[pltpu.VMEM((rows, 2 * GRID_W, LANES), BF16),
                        pltpu.VMEM((rows, 2 * GRID_W, LANES), BF16)],
        compiler_params=_params(("arbitrary", "arbitrary"), bufs),
        name="nbr_attn",
    )(q_rot, q_plain, k_rot, v, kc, vc, bias_tab)


def _bias_table(rpb):
    c = np.arange(GRID_W)
    col_start = np.clip(c - NA_COLS // 2, 0, GRID_W - NA_COLS)
    in_win = (c[None, :] >= col_start[:, None]) & (c[None, :] < col_start[:, None] + NA_COLS)
    dc = np.clip(c[None, :] - c[:, None], -(NA_COLS - 1), NA_COLS - 1) + (NA_COLS - 1)
    select = (dc[None] == np.arange(2 * NA_COLS - 1)[:, None, None]).astype(np.float32)
    toe = jnp.einsum("hdm,mqk->hdqk", rpb, jnp.asarray(select), precision=lax.Precision.HIGHEST)
    toe = jnp.where(jnp.asarray(in_win), toe, NEG_INF)
    return jnp.concatenate([toe, toe], axis=-1)


def _merge_kernel(x_ref, yr_ref, yn_ref, sh_ref, sc_ref, gate_ref, g_ref,
                  wgr_ref, wgn_ref, wr_ref, wn_ref, wo_ref, o_ref):
    x = x_ref[...]
    xn = _modulate(x, g_ref[...], sh_ref[0], sc_ref[0]).astype(BF16)
    z_r = jnp.dot(yr_ref[...], wr_ref[...], preferred_element_type=F32)
    z_n = jnp.dot(yn_ref[...], wn_ref[...], preferred_element_type=F32)
    g_r = jax.nn.sigmoid(jnp.dot(xn, wgr_ref[...], preferred_element_type=F32))
    g_n = jax.nn.sigmoid(jnp.dot(xn, wgn_ref[...], preferred_element_type=F32))
    merged = g_r * z_r + g_n * z_n
    o_ref[...] = x + gate_ref[0] * jnp.dot(merged.astype(BF16), wo_ref[...], preferred_element_type=F32)


def _merge_call(x, y_rnn, y_na, mods3, g_mix, w_in, w_rnn_out, w_na_out, w_out):
    b, t, d = x.shape
    tm = TOK_TILE
    tok = pl.BlockSpec((None, tm, d), lambda bi, i: (bi, i, 0))
    mod = lambda k: pl.BlockSpec((1, 1, D_MODEL), lambda bi, i: (bi * N_MOD + k, 0, 0))
    wblk = lambda c: pl.BlockSpec((D_MODEL, D_MODEL), lambda bi, i: (0, c))
    bufs = [((tm, d), F32, 4), ((tm, d), BF16, 4), ((D_MODEL, D_MODEL), BF16, 10), ((tm, d), F32, 4)]
    return pl.pallas_call(
        _merge_kernel,
        grid=(b, t // tm),
        in_specs=[tok, tok, tok, mod(0), mod(1), mod(2),
                  pl.BlockSpec((1, D_MODEL), lambda bi, i: (0, 0)),
                  wblk(COL_MR), wblk(COL_MN), wblk(0), wblk(0), wblk(0)],
        out_specs=tok,
        out_shape=jax.ShapeDtypeStruct((b, t, d), F32),
        compiler_params=_params(("arbitrary", "arbitrary"), bufs),
        name="merge",
    )(x, y_rnn, y_na, mods3, mods3, mods3, g_mix, w_in, w_in, w_rnn_out, w_na_out, w_out)


def _ffn_kernel(x_ref, xp_ref, xnx_ref, sh_ref, sc_ref, gate_ref, g_ref, wup_ref, cw_ref, cb_ref,
                wdn_ref, o_ref, act_s):
    tm = x_ref.shape[0]
    i = pl.program_id(1)
    x = x_ref[...]
    xe = jnp.concatenate([xp_ref[...], x, xnx_ref[...]], axis=0)
    xn = _modulate(xe, g_ref[...], sh_ref[0], sc_ref[0])
    row = lax.broadcasted_iota(jnp.int32, (tm + 2 * HALO, 1), 0)
    outside = ((row < HALO) & (i == 0)) | ((row >= tm + HALO) & (i == pl.num_programs(1) - 1))
    xn = jnp.where(outside, 0.0, xn).astype(BF16)
    cw = cw_ref[...]
    cb = cb_ref[...]
    ts = tm // FFN_SUBTILES
    for s in range(FFN_SUBTILES):
        xs = xn[s * ts:s * ts + ts + 2 * HALO]
        for c in range(D_FF // FF_CHUNK):
            halves = []
            for half in range(2):
                sl = slice(half * D_FF + c * FF_CHUNK, half * D_FF + (c + 1) * FF_CHUNK)
                u = jnp.dot(xs, wup_ref[:, sl], preferred_element_type=F32)
                acc = cb[:, sl]
                for k in range(FFN_CONV_W):
                    off = HALO - FFN_CONV_LEFT + k
                    acc = acc + cw[k:k + 1, sl] * u[off:off + ts]
                halves.append(acc)
            a, g = halves
            act_s[s, :, c * FF_CHUNK:(c + 1) * FF_CHUNK] = ((a * jax.nn.sigmoid(a)) * g).astype(BF16)
        rows = slice(s * ts, (s + 1) * ts)
        o_ref[rows, :] = x[rows] + gate_ref[0] * jnp.dot(act_s[s], wdn_ref[...], preferred_element_type=F32)


def _ffn_call(x1, mods3, g_ffn, w_up, conv_w, conv_b, w_down):
    b, t, d = x1.shape
    tm = TOK_TILE
    hb = tm // HALO
    n_halo_blocks = t // HALO
    tok = pl.BlockSpec((None, tm, d), lambda bi, i: (bi, i, 0))
    prev = pl.BlockSpec((None, HALO, d), lambda bi, i: (bi, jnp.maximum(i * hb - 1, 0), 0))
    nxt = pl.BlockSpec((None, HALO, d), lambda bi, i: (bi, jnp.minimum((i + 1) * hb, n_halo_blocks - 1), 0))
    mod = lambda k: pl.BlockSpec((1, 1, D_MODEL), lambda bi, i: (bi * N_MOD + k, 0, 0))
    const = lambda shape: pl.BlockSpec(shape, lambda bi, i: (0,) * len(shape),
                                       pipeline_mode=pl.Buffered(1))
    bufs = [((tm, d), F32, 4), ((D_MODEL, 2 * D_FF), BF16, 1), ((D_FF, D_MODEL), BF16, 1),
            ((tm, D_FF), BF16, 1), ((tm + 2 * HALO, d), F32, 2)]
    return pl.pallas_call(
        _ffn_kernel,
        grid=(b, t // tm),
        in_specs=[tok, prev, nxt, mod(3), mod(4), mod(5), const((1, D_MODEL)),
                  const((D_MODEL, 2 * D_FF)), const((FFN_CONV_W, 2 * D_FF)), const((1, 2 * D_FF)),
                  const((D_FF, D_MODEL))],
        out_specs=tok,
        out_shape=jax.ShapeDtypeStruct((b, t, d), F32),
        scratch_shapes=[pltpu.VMEM((FFN_SUBTILES, tm // FFN_SUBTILES, D_FF), BF16)],
        compiler_params=_params(("arbitrary", "arbitrary"), bufs),
        name="conv_ffn",
    )(x1, x1, x1, mods3, mods3, mods3, g_ffn, w_up, conv_w, conv_b, w_down)


def _rope_tables(t):
    n = HEAD_DIM // 4
    freq = (ROPE_BASE ** (-np.arange(n, dtype=np.float32) / n)).astype(np.float32)
    pos = np.arange(t)
    ang_r = (pos // GRID_W).astype(np.float32)[:, None] * freq
    ang_c = (pos % GRID_W).astype(np.float32)[:, None] * freq
    cos = np.concatenate([np.cos(ang_r)] * 2 + [np.cos(ang_c)] * 2, axis=-1)
    sin = np.concatenate([-np.sin(ang_r), np.sin(ang_r), -np.sin(ang_c), np.sin(ang_c)], axis=-1)
    reps = LANES // HEAD_DIM
    return (jnp.asarray(np.tile(cos, (1, reps)), F32), jnp.asarray(np.tile(sin, (1, reps)), F32))


def kernel(x, c, ctx, c_ctx, w_mod, b_mod, norm_mix_g, norm_ffn_g, w_in, lru_conv_w, lru_conv_b, lru_wa, lru_ba, lru_wx, lru_bx, lru_lambda, q_norm_g, k_norm_g, na_rpb, w_rnn_out, w_na_out, w_out, w_up, ffn_conv_w, ffn_conv_b, w_down):
    depth = w_mod.shape[0]
    assert depth == 1, "single-layer block"
    b, t, d = x.shape
    assert d == D_MODEL and b < MOD_ROWS and t % TOK_TILE == 0 and t % GRID_W == 0
    l = 0

    cc = jnp.zeros((MOD_ROWS, d), F32).at[:b].set(c).at[b].set(c_ctx)
    w_in_b = w_in[l].astype(BF16)
    g_mix = norm_mix_g[l][None, :]
    g_ffn = norm_ffn_g[l][None, :]
    q_gain = jnp.tile(q_norm_g[l], NA_HEADS)[None, :]
    k_gain = jnp.tile(k_norm_g[l], NA_HEADS)[None, :]
    head_of = np.arange(MXU_DIM) // HEAD_DIM
    e_mat = jnp.asarray(np.where(head_of[:, None] == head_of[None, :], 1.0 / HEAD_DIM, 0.0), BF16)
    cos_t, sin_t = _rope_tables(t)
    wcat = jnp.concatenate([lru_wa[l], lru_wx[l]], axis=-1).astype(BF16)
    bcat = jnp.stack([lru_ba[l, 0], lru_bx[l, 0], lru_ba[l, 1], lru_bx[l, 1]])
    bias_tab = _bias_table(na_rpb[l])

    mods = _mod_call(cc, w_mod[l], b_mod[l][None, :])
    mods3 = mods.reshape(MOD_ROWS * N_MOD, 1, d)

    k_rot, v, q_rot, q_plain = _qkv_call(x, mods3, 0, g_mix, w_in_b, k_gain, q_gain, e_mat,
                                         cos_t, sin_t, latent=True)
    kc, vc = _qkv_call(ctx, mods3, b, g_mix, w_in_b, k_gain, None, e_mat, None, None, latent=False)
    y_rnn = _lru_call(x, ctx, mods3, g_mix, w_in_b, lru_conv_w[l], lru_conv_b[l][None, :],
                      wcat, bcat, lru_lambda[l])
    y_na = _attn_call(q_rot, q_plain, k_rot, v, kc, vc, bias_tab)
    x1 = _merge_call(x, y_rnn, y_na, mods3, g_mix, w_in_b, w_rnn_out[l].astype(BF16),
                     w_na_out[l].astype(BF16), w_out[l].astype(BF16))
    return _ffn_call(x1, mods3, g_ffn, w_up[l].astype(BF16), ffn_conv_w[l], ffn_conv_b[l][None, :],
                     w_down[l].astype(BF16))
```

```python
import functools

import jax
import jax.numpy as jnp
import numpy as np
from jax import lax
from jax.experimental import pallas as pl
from jax.experimental.pallas import tpu as pltpu

F32 = jnp.float32
BF16 = jnp.bfloat16

D_MODEL = 1024
GRID_W = 64
D_RNN = D_MODEL
LRU_BLOCKS = 8
LRU_BLOCK_W = D_RNN // LRU_BLOCKS
LRU_CONV_W = 4
LRU_CONV_LEFT = 2
LRU_C = 8.0
NA_HEADS = 16
HEAD_DIM = 64
D_ATT = NA_HEADS * HEAD_DIM
NA_ROWS = 8
NA_COLS = 16
ROPE_BASE = 10000.0
D_FF = ((8 * D_MODEL // 3 + 127) // 128) * 128
FFN_CONV_W = 3
FFN_CONV_LEFT = 1
N_MOD = 6
EPS = 1e-6
NEG_INF = -1e30

COL_XR, COL_K, COL_V, COL_GX, COL_Q, COL_MR, COL_MN = range(7)

LANES = 128
SUBLANES = 8
MXU_DIM = 256
VMEM_LIMIT_CAP = 60000 * 1024
VMEM_TEMP_HEADROOM = 12 * 1024 * 1024

MOD_ROWS = 16
TOK_TILE = 512
LRU_CG = 256
LRU_CHUNK = 128
HALO = SUBLANES
FF_CHUNK = MXU_DIM
FFN_TILE = 512
FFN_SUBTILES = 2
QKV_SUBTILES = 2
PERM_ROWS = 128
ATTN_GROUP = 4


def _nbytes(shape, dtype):
    n = 1
    for s in shape:
        n *= s
    return n * jnp.dtype(dtype).itemsize


def _vmem_limit(buffers):
    need = sum(_nbytes(s, d) * c for s, d, c in buffers) + VMEM_TEMP_HEADROOM
    return int(min(need, VMEM_LIMIT_CAP))


def _params(semantics, buffers):
    return pltpu.CompilerParams(dimension_semantics=semantics,
                                vmem_limit_bytes=_vmem_limit(buffers))


def _modulate(x, g, shift, scale):
    ms = jnp.mean(x * x, axis=-1, keepdims=True)
    y = x * lax.rsqrt(ms + EPS)
    return (y * g) * (1.0 + scale) + shift


def _mod_kernel(cc_ref, w_ref, b_ref, o_ref):
    s = cc_ref[...]
    s = s * jax.nn.sigmoid(s)
    o_ref[...] = jnp.dot(s.astype(BF16), w_ref[...].astype(BF16),
                         preferred_element_type=F32) + b_ref[...]


def _mod_call(cc, w_mod, b_mod):
    n = w_mod.shape[1]
    tn = D_MODEL
    bufs = [((MOD_ROWS, D_MODEL), F32, 2), ((D_MODEL, tn), F32, 2), ((MOD_ROWS, tn), F32, 2)]
    return pl.pallas_call(
        _mod_kernel,
        grid=(n // tn,),
        in_specs=[pl.BlockSpec((MOD_ROWS, D_MODEL), lambda j: (0, 0)),
                  pl.BlockSpec((D_MODEL, tn), lambda j: (0, j)),
                  pl.BlockSpec((1, tn), lambda j: (0, j))],
        out_specs=pl.BlockSpec((MOD_ROWS, tn), lambda j: (0, j)),
        out_shape=jax.ShapeDtypeStruct((MOD_ROWS, n), F32),
        compiler_params=_params(("arbitrary",), bufs),
        name="adaln_mod",
    )(cc, w_mod, b_mod)


def _head_rms(t, gain, e):
    sq = t * t
    hi = sq.astype(BF16)
    lo = (sq - hi.astype(F32)).astype(BF16)
    ms = jnp.dot(hi, e, preferred_element_type=F32) + jnp.dot(lo, e, preferred_element_type=F32)
    return (t * lax.rsqrt(ms + EPS)) * gain


def _rope(y, cos, sin_signed, first_half):
    partner = jnp.where(first_half, pltpu.roll(y, LANES - 16, 1), pltpu.roll(y, 16, 1))
    return y * cos + partner * sin_signed


def _qkv_kernel(*refs, latent):
    if latent:
        (x_ref, sh_ref, sc_ref, g_ref, wk_ref, wv_ref, wq_ref, kg_ref, qg_ref, e_ref,
         cos_ref, sin_ref, k_out, v_out, qr_out, qp_out) = refs
    else:
        (x_ref, sh_ref, sc_ref, g_ref, wk_ref, wv_ref, kg_ref, e_ref, k_out, v_out) = refs
    e = e_ref[...]
    scale = HEAD_DIM ** -0.5
    tm = x_ref.shape[0]
    ts = tm // QKV_SUBTILES
    for s in range(QKV_SUBTILES):
        rows = slice(s * ts, (s + 1) * ts)
        xn = _modulate(x_ref[rows, :], g_ref[...], sh_ref[0], sc_ref[0]).astype(BF16)
        if latent:
            cos = cos_ref[rows, :]
            sin = sin_ref[rows, :]
            lane = lax.broadcasted_iota(jnp.int32, cos.shape, 1)
            first_half = (lane % 32) < 16
        for j in range(D_ATT // MXU_DIM):
            sl = slice(MXU_DIM * j, MXU_DIM * (j + 1))
            v_out[rows, sl] = jnp.dot(xn, wv_ref[:, sl], preferred_element_type=F32).astype(BF16)
            k = _head_rms(jnp.dot(xn, wk_ref[:, sl], preferred_element_type=F32), kg_ref[:, sl], e)
            if not latent:
                k_out[rows, sl] = k.astype(BF16)
                continue
            q = _head_rms(jnp.dot(xn, wq_ref[:, sl], preferred_element_type=F32), qg_ref[:, sl], e)
            qp_out[rows, sl] = (q * scale).astype(BF16)
            for h in range(MXU_DIM // LANES):
                hl = slice(LANES * h, LANES * (h + 1))
                ol = slice(MXU_DIM * j + LANES * h, MXU_DIM * j + LANES * (h + 1))
                k_out[rows, ol] = _rope(k[:, hl], cos, sin, first_half).astype(BF16)
                qr_out[rows, ol] = (_rope(q[:, hl], cos, sin, first_half) * scale).astype(BF16)


def _qkv_call(x, mods3, mod_row0, g_mix, w_in, k_gain, q_gain, e_mat, cos_t, sin_t, latent):
    b, t, d = x.shape
    tm = min(TOK_TILE, t)
    nt = t // tm
    wblk = lambda c: pl.BlockSpec((D_MODEL, D_MODEL), lambda bi, i: (0, c))
    vec = pl.BlockSpec((1, D_MODEL), lambda bi, i: (0, 0))
    if latent:
        mod = lambda k: pl.BlockSpec((1, 1, D_MODEL), lambda bi, i: (bi * N_MOD + k, 0, 0))
    else:
        mod = lambda k: pl.BlockSpec((1, 1, D_MODEL), lambda bi, i: (mod_row0 * N_MOD + k, 0, 0))
    tok = pl.BlockSpec((None, tm, d), lambda bi, i: (bi, i, 0))
    in_specs = [tok, mod(0), mod(1), vec, wblk(COL_K), wblk(COL_V)]
    args = [x, mods3, mods3, g_mix, w_in, w_in]
    if latent:
        in_specs += [wblk(COL_Q), vec, vec]
        args += [w_in, k_gain, q_gain]
    else:
        in_specs += [vec]
        args += [k_gain]
    in_specs += [pl.BlockSpec((MXU_DIM, MXU_DIM), lambda bi, i: (0, 0))]
    args += [e_mat]
    n_out = 2
    if latent:
        rope_spec = pl.BlockSpec((tm, LANES), lambda bi, i: (i, 0))
        in_specs += [rope_spec, rope_spec]
        args += [cos_t, sin_t]
        n_out = 4
    out_sd = jax.ShapeDtypeStruct((b, t, D_ATT), BF16)
    bufs = [((tm, d), F32, 2), ((D_MODEL, D_MODEL), BF16, 6), ((tm, D_ATT), BF16, 2 * n_out),
            ((tm, LANES), F32, 4), ((tm, d), BF16, 1)]
    return pl.pallas_call(
        functools.partial(_qkv_kernel, latent=latent),
        grid=(b, nt),
        in_specs=in_specs,
        out_specs=[tok] * n_out,
        out_shape=[out_sd] * n_out,
        compiler_params=_params(("arbitrary", "arbitrary"), bufs),
        name="qkv_proj" if latent else "ctx_proj",
    )(*args)


def _softplus(x):
    return jnp.maximum(x, 0.0) + jnp.log1p(jnp.exp(-jnp.abs(x)))


SEG_BLOCK = SUBLANES * SUBLANES
LRU_HALO_FRONT = 2 * SUBLANES
LRU_HALO_BACK = SUBLANES


def _seg_permutation(n):
    p = np.arange(n)
    src = (p // SEG_BLOCK) * SEG_BLOCK + (p % SUBLANES) * SUBLANES + (p % SEG_BLOCK) // SUBLANES
    return jnp.asarray(src[:, None] == p[None, :], BF16)


def _seg_permute_rows(perm, x):
    n = perm.shape[0]
    parts = [jnp.dot(perm, x[n * i:n * (i + 1)], preferred_element_type=F32) for i in range(x.shape[0] // n)]
    return jnp.concatenate(parts, axis=0).astype(BF16)


def _seg_conv(x, before, after, taps, bias, left):
    n, c = x.shape
    right = len(taps) - 1 - left
    nb = n // SEG_BLOCK
    tiles = [x[SUBLANES * i:SUBLANES * (i + 1)] for i in range(n // SUBLANES)]
    sub = lax.broadcasted_iota(jnp.int32, (SUBLANES, c), 0)
    out = []
    for blk in range(nb):
        cur = tiles[SUBLANES * blk:SUBLANES * (blk + 1)]
        down, up = {}, {}
        for e in range(1, left + 1):
            j = SUBLANES - e
            prev = tiles[SUBLANES * (blk - 1) + j] if blk > 0 else before[left - e]
            down[j] = jnp.where(sub == 0, pltpu.roll(prev, 1, 0), pltpu.roll(cur[j], 1, 0))
        for e in range(right):
            nxt = tiles[SUBLANES * (blk + 1) + e] if blk + 1 < nb else after[e]
            up[e] = jnp.where(sub == SUBLANES - 1, pltpu.roll(nxt, SUBLANES - 1, 0),
                              pltpu.roll(cur[e], SUBLANES - 1, 0))
        for j in range(SUBLANES):
            acc = bias
            for k, wk in enumerate(taps):
                src = j + k - left
                if src < 0:
                    tile = down[src + SUBLANES]
                elif src >= SUBLANES:
                    tile = up[src - SUBLANES]
                else:
                    tile = cur[src]
                acc = acc + wk * tile
            out.append(acc)
    return jnp.concatenate(out, axis=0)


def _lru_coeffs(xc, wcat_ref, bcat, sp8, d):
    xcb = xc.astype(BF16)
    rs, gs = [], []
    for blk in range(LRU_CG // LRU_BLOCK_W):
        bl = slice(LRU_BLOCK_W * blk, LRU_BLOCK_W * (blk + 1))
        z = jnp.dot(xcb[:, bl], wcat_ref[d, blk], preferred_element_type=F32)
        rs.append(z[:, :LRU_BLOCK_W])
        gs.append(z[:, LRU_BLOCK_W:])
    r = jax.nn.sigmoid(jnp.concatenate(rs, axis=-1) + bcat[2 * d:2 * d + 1, :])
    gate = jax.nn.sigmoid(jnp.concatenate(gs, axis=-1) + bcat[2 * d + 1:2 * d + 2, :])
    a = jnp.exp(r * -sp8[d:d + 1, :])
    w = jnp.tanh(r * sp8[d:d + 1, :]) * (a * a + 1.0)
    root = jnp.where(w > 0.0, w * lax.rsqrt(w), 0.0)
    b = root * gate * xc
    return a, b


def _seg_scan(a, b, carry, reverse, want_h=True):
    n, c = a.shape
    nb = n // SEG_BLOCK
    sub = lax.broadcasted_iota(jnp.int32, (SUBLANES, c), 0)
    steps = range(SUBLANES - 1, -1, -1) if reverse else range(SUBLANES)
    hs = [None] * (nb * SUBLANES)
    for blk in (range(nb - 1, -1, -1) if reverse else range(nb)):
        tile = lambda x, j: x[blk * SEG_BLOCK + j * SUBLANES:blk * SEG_BLOCK + (j + 1) * SUBLANES]
        acc_a, acc_h = {}, {}
        prev = None
        for j in steps:
            aj, bj = tile(a, j), tile(b, j)
            acc_h[j] = bj if prev is None else aj * acc_h[prev] + bj
            acc_a[j] = aj if prev is None else aj * acc_a[prev]
            prev = j
        pa, ph = acc_a[prev], acc_h[prev]
        for s in (1, 2, 4):
            shift = SUBLANES - s if reverse else s
            m = (sub < SUBLANES - s) if reverse else (sub >= s)
            ph = jnp.where(m, pa * pltpu.roll(ph, shift, 0) + ph, ph)
            pa = jnp.where(m, pa * pltpu.roll(pa, shift, 0), pa)
        end_state = pa * carry + ph
        if reverse:
            seg_in = jnp.where(sub == SUBLANES - 1, carry, pltpu.roll(end_state, SUBLANES - 1, 0))
            carry = end_state[0:1]
        else:
            seg_in = jnp.where(sub == 0, carry, pltpu.roll(end_state, 1, 0))
            carry = end_state[SUBLANES - 1:SUBLANES]
        if want_h:
            for j in steps:
                hs[blk * SUBLANES + j] = acc_h[j] + acc_a[j] * seg_in
    return (jnp.concatenate(hs, axis=0) if want_h else None), carry


def _lru_kernel(x_ref, ctx_ref, shx_ref, scx_ref, shc_ref, scc_ref, g_ref, perm_ref, wxr_ref, wgx_ref,
                cw_ref, cb_ref, wcat_ref, bcat_ref, lam_ref, y_ref,
                xn_s, cn_s, xr_s, cr_s, xc_s, hf_s):
    t = x_ref.shape[0]
    nctx = ctx_ref.shape[0]
    cg = y_ref.shape[1]
    perm = perm_ref[...]

    def permuted_norm(rows_f32, shift, scale):
        return _seg_permute_rows(perm, _modulate(rows_f32, g_ref[...], shift, scale).astype(BF16))

    @pl.when(pl.program_id(1) == 0)
    def _():
        def body(i, _):
            r0 = pl.multiple_of(i * TOK_TILE, TOK_TILE)
            xn_s[pl.ds(r0, TOK_TILE), :] = permuted_norm(x_ref[pl.ds(r0, TOK_TILE), :], shx_ref[0], scx_ref[0])
            return 0
        lax.fori_loop(0, t // TOK_TILE, body, 0)
        cn_s[...] = permuted_norm(ctx_ref[...], shc_ref[0], scc_ref[0])

    xr_s[0:LRU_HALO_FRONT, :] = jnp.zeros((LRU_HALO_FRONT, cg), F32)
    xr_s[t + LRU_HALO_FRONT:, :] = jnp.zeros((LRU_HALO_BACK, cg), F32)
    cr_s[0:LRU_HALO_FRONT, :] = jnp.zeros((LRU_HALO_FRONT, cg), F32)
    cr_s[nctx + LRU_HALO_FRONT:, :] = jnp.zeros((LRU_HALO_BACK, cg), F32)

    def project_xr(blk):
        rows = slice(blk * TOK_TILE, (blk + 1) * TOK_TILE)
        xr_s[LRU_HALO_FRONT + blk * TOK_TILE:LRU_HALO_FRONT + (blk + 1) * TOK_TILE, :] = jnp.dot(
            xn_s[rows, :], wxr_ref[...], preferred_element_type=F32)
    cr_s[LRU_HALO_FRONT:LRU_HALO_FRONT + nctx, :] = jnp.dot(cn_s[...], wxr_ref[...],
                                                            preferred_element_type=F32)
    project_xr(0)

    conv_w = cw_ref[...]
    taps = [conv_w[k:k + 1, :] for k in range(LRU_CONV_W)]
    conv_b = cb_ref[...]
    bcat = bcat_ref[...]
    sp8 = LRU_C * _softplus(-lam_ref[...])
    chunk_ext = LRU_CHUNK + LRU_HALO_FRONT + LRU_HALO_BACK

    def conv(xe):
        before = [xe[SUBLANES * i:SUBLANES * (i + 1)] for i in range(LRU_HALO_FRONT // SUBLANES)]
        after = [xe[LRU_HALO_FRONT + LRU_CHUNK:]]
        return _seg_conv(xe[LRU_HALO_FRONT:LRU_HALO_FRONT + LRU_CHUNK], before, after, taps, conv_b,
                         LRU_CONV_LEFT)
    coeffs = functools.partial(_lru_coeffs, wcat_ref=wcat_ref, bcat=bcat, sp8=sp8)
    n_ctx = nctx // LRU_CHUNK
    ctx_xc = [conv(cr_s[i * LRU_CHUNK:i * LRU_CHUNK + chunk_ext, :]) for i in range(n_ctx)]

    carry = jnp.zeros((1, cg), F32)
    for i in range(n_ctx):
        a, b = coeffs(ctx_xc[i], d=0)
        _, carry = _seg_scan(a, b, carry, reverse=False, want_h=False)

    chunks_per_blk = TOK_TILE // LRU_CHUNK
    n_blk = t // TOK_TILE
    for blk in range(n_blk):
        if blk + 1 < n_blk:
            project_xr(blk + 1)
        for c in range(chunks_per_blk):
            r0 = (blk * chunks_per_blk + c) * LRU_CHUNK
            xc = conv(xr_s[r0:r0 + chunk_ext, :])
            xc_s[r0:r0 + LRU_CHUNK, :] = xc
            a, b = coeffs(xc, d=0)
            h, carry = _seg_scan(a, b, carry, reverse=False)
            hf_s[r0:r0 + LRU_CHUNK, :] = h

    carry = jnp.zeros((1, cg), F32)
    for i in range(n_ctx - 1, -1, -1):
        a, b = coeffs(ctx_xc[i], d=1)
        _, carry = _seg_scan(a, b, carry, reverse=True, want_h=False)

    def store_unpermuted(blk, y_perm):
        y_ref[blk * TOK_TILE:(blk + 1) * TOK_TILE, :] = _seg_permute_rows(perm, y_perm)

    pending = None
    for blk in range(n_blk - 1, -1, -1):
        gate = jax.nn.gelu(jnp.dot(xn_s[blk * TOK_TILE:(blk + 1) * TOK_TILE, :], wgx_ref[...],
                                   preferred_element_type=F32))
        ys = [None] * chunks_per_blk
        for c in range(chunks_per_blk - 1, -1, -1):
            r0 = (blk * chunks_per_blk + c) * LRU_CHUNK
            a, b = coeffs(xc_s[r0:r0 + LRU_CHUNK, :], d=1)
            h, carry = _seg_scan(a, b, carry, reverse=True)
            ys[c] = ((hf_s[r0:r0 + LRU_CHUNK, :] + h)
                     * gate[c * LRU_CHUNK:(c + 1) * LRU_CHUNK]).astype(BF16)
        if pending is not None:
            store_unpermuted(*pending)
        pending = (blk, jnp.concatenate(ys, axis=0))
    store_unpermuted(*pending)


def _lru_call(x, ctx, mods3, g_mix, w_in, conv_w, conv_b, wcat, bcat, lam):
    b, t, d = x.shape
    nctx = ctx.shape[1]
    cg = LRU_CG
    ng = D_RNN // cg
    ctx_row = b * N_MOD
    mod = lambda k: pl.BlockSpec((1, 1, D_MODEL), lambda bi, g: (bi * N_MOD + k, 0, 0))
    modc = lambda k: pl.BlockSpec((1, 1, D_MODEL), lambda bi, g: (ctx_row + k, 0, 0))
    in_specs = [
        pl.BlockSpec((None, t, d), lambda bi, g: (bi, 0, 0)),
        pl.BlockSpec((None, nctx, d), lambda bi, g: (bi, 0, 0)),
        mod(0), mod(1), modc(0), modc(1),
        pl.BlockSpec((1, D_MODEL), lambda bi, g: (0, 0)),
        pl.BlockSpec((PERM_ROWS, PERM_ROWS), lambda bi, g: (0, 0)),
        pl.BlockSpec((D_MODEL, cg), lambda bi, g: (0, COL_XR * ng + g)),
        pl.BlockSpec((D_MODEL, cg), lambda bi, g: (0, COL_GX * ng + g)),
        pl.BlockSpec((LRU_CONV_W, cg), lambda bi, g: (0, g)),
        pl.BlockSpec((1, cg), lambda bi, g: (0, g)),
        pl.BlockSpec((2, cg // LRU_BLOCK_W, LRU_BLOCK_W, 2 * LRU_BLOCK_W), lambda bi, g: (0, g, 0, 0)),
        pl.BlockSpec((4, cg), lambda bi, g: (0, g)),
        pl.BlockSpec((2, cg), lambda bi, g: (0, g)),
    ]
    halo = LRU_HALO_FRONT + LRU_HALO_BACK
    scratch = [
        pltpu.VMEM((t, d), BF16), pltpu.VMEM((nctx, d), BF16),
        pltpu.VMEM((t + halo, cg), F32), pltpu.VMEM((nctx + halo, cg), F32),
        pltpu.VMEM((t, cg), F32), pltpu.VMEM((t, cg), F32),
    ]
    bufs = [((t, d), F32, 2), ((nctx, d), F32, 2), ((D_MODEL, cg), BF16, 4), ((t, cg), BF16, 2),
            ((t, d), BF16, 1), ((nctx, d), BF16, 1), ((t + halo, cg), F32, 3),
            ((nctx + halo, cg), F32, 1)]
    return pl.pallas_call(
        _lru_kernel,
        grid=(b, ng),
        in_specs=in_specs,
        out_specs=pl.BlockSpec((None, t, cg), lambda bi, g: (bi, 0, g)),
        out_shape=jax.ShapeDtypeStruct((b, t, D_RNN), BF16),
        scratch_shapes=scratch,
        compiler_params=_params(("arbitrary", "arbitrary"), bufs),
        name="rglru",
    )(x, ctx, mods3, mods3, mods3, mods3, g_mix, _seg_permutation(PERM_ROWS), w_in, w_in,
      conv_w, conv_b, wcat, bcat, lam)


NT_DIMS = (((1,), (1,)), ((), ()))
KEY_ROWS_PER_TILE = LANES // GRID_W


def _attn_group_geometry(g, rows):
    r0 = g * ATTN_GROUP
    starts = [min(max(r0 + o - NA_ROWS // 2, 0), rows - NA_ROWS) for o in range(ATTN_GROUP)]
    key_row0 = starts[0]
    n_key_rows = -(-(starts[-1] + NA_ROWS - key_row0) // KEY_ROWS_PER_TILE) * KEY_ROWS_PER_TILE
    desc = tuple((starts[o] - key_row0, key_row0 - (r0 + o)) for o in range(ATTN_GROUP))
    return key_row0, n_key_rows, desc


BIAS_BOTH, BIAS_RIGHT_ONLY, BIAS_LEFT_ONLY = range(3)


def _bias_tile_index(off, base, tile):
    left = KEY_ROWS_PER_TILE * tile
    in_window = lambda j: off <= j < off + NA_ROWS
    dr_left = base + left + (NA_ROWS - 1)
    if in_window(left) and in_window(left + 1):
        return BIAS_BOTH, dr_left
    if in_window(left + 1):
        return BIAS_RIGHT_ONLY, dr_left + 1
    return BIAS_LEFT_ONLY, dr_left


def _attn_kernel(qr_ref, qp_ref, k_ref, v_ref, kc_ref, vc_ref, bias_ref, y_ref, qs_s, qps_s):
    t = k_ref.shape[0]
    rows = t // GRID_W
    lane = lax.broadcasted_iota(jnp.int32, (rows, GRID_W, LANES), 2)
    head_a = lane < HEAD_DIM

    def stack_heads(q_ref, dst):
        q3 = q_ref[...].reshape(rows, GRID_W, LANES)
        zero = jnp.zeros_like(q3)
        dst[...] = jnp.concatenate([jnp.where(head_a, q3, zero), jnp.where(head_a, zero, q3)], axis=1)
    stack_heads(qr_ref, qs_s)
    stack_heads(qp_ref, qps_s)

    kc = kc_ref[...]
    vc = vc_ref[...]
    out_a = lax.broadcasted_iota(jnp.int32, (GRID_W, LANES), 1) < HEAD_DIM
    rows_stacked = 2 * GRID_W

    def lane_tiles(arrays):
        return [a[:, LANES * i:LANES * (i + 1)] for a in arrays for i in range(a.shape[1] // LANES)]

    for g in range(rows // ATTN_GROUP):
        r0 = g * ATTN_GROUP
        key_row0, n_key_rows, desc = _attn_group_geometry(g, rows)
        k_sup = k_ref[key_row0 * GRID_W:(key_row0 + n_key_rows) * GRID_W, :]
        v_sup = v_ref[key_row0 * GRID_W:(key_row0 + n_key_rows) * GRID_W, :]
        n_tiles = n_key_rows // KEY_ROWS_PER_TILE
        q_grp = qs_s[r0:r0 + ATTN_GROUP].reshape(ATTN_GROUP * rows_stacked, LANES)
        qp_grp = qps_s[r0:r0 + ATTN_GROUP].reshape(ATTN_GROUP * rows_stacked, LANES)
        s_lat = lax.dot_general(q_grp, k_sup, NT_DIMS, preferred_element_type=F32)
        s_ctx = lax.dot_general(qp_grp, kc, NT_DIMS, preferred_element_type=F32)
        p_rows, pc_rows, denoms = [], [], []
        for o in range(ATTN_GROUP):
            off, base = desc[o]
            lo, hi = off // KEY_ROWS_PER_TILE, -(-(off + NA_ROWS) // KEY_ROWS_PER_TILE)
            rsl = slice(o * rows_stacked, (o + 1) * rows_stacked)
            bias = []
            for tile in range(lo, hi):
                kind, dr = _bias_tile_index(off, base, tile)
                halves = []
                for head in range(LANES // HEAD_DIM):
                    if kind == BIAS_BOTH:
                        halves.append(bias_ref[head, dr])
                    elif kind == BIAS_LEFT_ONLY:
                        halves.append(jnp.where(out_a, bias_ref[head, dr], NEG_INF))
                    else:
                        halves.append(jnp.where(out_a, NEG_INF, bias_ref[head, dr - 1]))
                bias.append(jnp.concatenate(halves, axis=0))
            s_o = s_lat[rsl, lo * LANES:hi * LANES] + jnp.concatenate(bias, axis=-1)
            sc_o = s_ctx[rsl]
            m = jnp.max(functools.reduce(jnp.maximum, lane_tiles([s_o, sc_o])), axis=-1, keepdims=True)
            p_o = jnp.exp(s_o - m)
            pc_o = jnp.exp(sc_o - m)
            denoms.append(jnp.sum(functools.reduce(jnp.add, lane_tiles([p_o, pc_o])), axis=-1, keepdims=True))
            pieces = [p_o.astype(BF16)]
            if lo > 0:
                pieces.insert(0, jnp.zeros((rows_stacked, lo * LANES), BF16))
            if hi < n_tiles:
                pieces.append(jnp.zeros((rows_stacked, (n_tiles - hi) * LANES), BF16))
            p_rows.append(jnp.concatenate(pieces, axis=-1) if len(pieces) > 1 else pieces[0])
            pc_rows.append(pc_o.astype(BF16))
        o_grp = (jnp.dot(jnp.concatenate(p_rows, axis=0), v_sup, preferred_element_type=F32)
                 + jnp.dot(jnp.concatenate(pc_rows, axis=0), vc, preferred_element_type=F32))
        for o in range(ATTN_GROUP):
            blk = o_grp[o * rows_stacked:(o + 1) * rows_stacked] / denoms[o]
            y_ref[(r0 + o) * GRID_W:(r0 + o + 1) * GRID_W, :] = jnp.where(
                out_a, blk[:GRID_W], blk[GRID_W:]).astype(BF16)


def _attn_call(q_rot, q_plain, k_rot, v, kc, vc, bias_tab):
    b, t, _ = q_rot.shape
    nctx = kc.shape[1]
    rows = t // GRID_W
    npair = D_ATT // LANES
    tok = pl.BlockSpec((None, t, LANES), lambda bi, hp: (bi, 0, hp))
    ctx = pl.BlockSpec((None, nctx, LANES), lambda bi, hp: (bi, 0, hp))
    bias_shape = (LANES // HEAD_DIM,) + bias_tab.shape[1:]
    bufs = [((t, LANES), BF16, 10), ((nctx, LANES), BF16, 4), (bias_shape, F32, 2),
            ((rows, 2 * GRID_W, LANES), BF16, 2)]
    return pl.pallas_call(
        _attn_kernel,
        grid=(b, npair),
        in_specs=[tok, tok, tok, tok, ctx, ctx,
                  pl.BlockSpec(bias_shape, lambda bi, hp: (hp, 0, 0, 0))],
        out_specs=tok,
        out_shape=jax.ShapeDtypeStruct((b, t, D_ATT), BF16),
        scratch_shapes=[pltpu.VMEM((rows, 2 * GRID_W, LANES), BF16),
                        pltpu.VMEM((rows, 2 * GRID_W, LANES), BF16)],
        compiler_params=_params(("arbitrary", "arbitrary"), bufs),
        name="nbr_attn",
    )(q_rot, q_plain, k_rot, v, kc, vc, bias_tab)


def _bias_table(rpb):
    c = np.arange(GRID_W)
    col_start = np.clip(c - NA_COLS // 2, 0, GRID_W - NA_COLS)
    in_win = (c[None, :] >= col_start[:, None]) & (c[None, :] < col_start[:, None] + NA_COLS)
    dc = np.clip(c[None, :] - c[:, None], -(NA_COLS - 1), NA_COLS - 1) + (NA_COLS - 1)
    select = (dc[None] == np.arange(2 * NA_COLS - 1)[:, None, None]).astype(np.float32)
    toe = jnp.einsum("hdm,mqk->hdqk", rpb, jnp.asarray(select), precision=lax.Precision.HIGHEST)
    toe = jnp.where(jnp.asarray(in_win), toe, NEG_INF)
    nxt = jnp.concatenate([toe[:, 1:], jnp.full_like(toe[:, :1], NEG_INF)], axis=1)
    return jnp.concatenate([toe, nxt], axis=-1)


def _merge_kernel(x_ref, yr_ref, yn_ref, sh_ref, sc_ref, gate_ref, g_ref,
                  wgr_ref, wgn_ref, wr_ref, wn_ref, wo_ref, o_ref):
    x = x_ref[...]
    xn = _modulate(x, g_ref[...], sh_ref[0], sc_ref[0]).astype(BF16)
    z_r = jnp.dot(yr_ref[...], wr_ref[...], preferred_element_type=F32)
    z_n = jnp.dot(yn_ref[...], wn_ref[...], preferred_element_type=F32)
    g_r = jax.nn.sigmoid(jnp.dot(xn, wgr_ref[...], preferred_element_type=F32))
    g_n = jax.nn.sigmoid(jnp.dot(xn, wgn_ref[...], preferred_element_type=F32))
    merged = g_r * z_r + g_n * z_n
    o_ref[...] = x + gate_ref[0] * jnp.dot(merged.astype(BF16), wo_ref[...], preferred_element_type=F32)


def _merge_call(x, y_rnn, y_na, mods3, g_mix, w_in, w_rnn_out, w_na_out, w_out):
    b, t, d = x.shape
    tm = TOK_TILE
    tok = pl.BlockSpec((None, tm, d), lambda bi, i: (bi, i, 0))
    mod = lambda k: pl.BlockSpec((1, 1, D_MODEL), lambda bi, i: (bi * N_MOD + k, 0, 0))
    wblk = lambda c: pl.BlockSpec((D_MODEL, D_MODEL), lambda bi, i: (0, c))
    bufs = [((tm, d), F32, 4), ((tm, d), BF16, 4), ((D_MODEL, D_MODEL), BF16, 10), ((tm, d), F32, 4)]
    return pl.pallas_call(
        _merge_kernel,
        grid=(b, t // tm),
        in_specs=[tok, tok, tok, mod(0), mod(1), mod(2),
                  pl.BlockSpec((1, D_MODEL), lambda bi, i: (0, 0)),
                  wblk(COL_MR), wblk(COL_MN), wblk(0), wblk(0), wblk(0)],
        out_specs=tok,
        out_shape=jax.ShapeDtypeStruct((b, t, d), F32),
        compiler_params=_params(("arbitrary", "arbitrary"), bufs),
        name="merge",
    )(x, y_rnn, y_na, mods3, mods3, mods3, g_mix, w_in, w_in, w_rnn_out, w_na_out, w_out)


def _ffn_kernel(x_ref, xp_ref, xnx_ref, sh_ref, sc_ref, gate_ref, g_ref, wup_ref, cw_ref, cb_ref,
                wdn_ref, o_ref, act_s):
    tm = x_ref.shape[0]
    i = pl.program_id(1)
    x = x_ref[...]
    xe = jnp.concatenate([xp_ref[...], x, xnx_ref[...]], axis=0)
    xn = _modulate(xe, g_ref[...], sh_ref[0], sc_ref[0])
    row = lax.broadcasted_iota(jnp.int32, (tm + 2 * HALO, 1), 0)
    outside = ((row < HALO) & (i == 0)) | ((row >= tm + HALO) & (i == pl.num_programs(1) - 1))
    xn = jnp.where(outside, 0.0, xn).astype(BF16)
    cw = cw_ref[...]
    cb = cb_ref[...]
    ts = tm // FFN_SUBTILES
    for s in range(FFN_SUBTILES):
        xs = xn[s * ts:s * ts + ts + 2 * HALO]
        for c in range(D_FF // FF_CHUNK):
            halves = []
            for half in range(2):
                sl = slice(half * D_FF + c * FF_CHUNK, half * D_FF + (c + 1) * FF_CHUNK)
                u = jnp.dot(xs, wup_ref[:, sl], preferred_element_type=F32)
                acc = cb[:, sl]
                for k in range(FFN_CONV_W):
                    off = HALO - FFN_CONV_LEFT + k
                    acc = acc + cw[k:k + 1, sl] * u[off:off + ts]
                halves.append(acc)
            a, g = halves
            act_s[s, :, c * FF_CHUNK:(c + 1) * FF_CHUNK] = ((a * jax.nn.sigmoid(a)) * g).astype(BF16)
        rows = slice(s * ts, (s + 1) * ts)
        o_ref[rows, :] = x[rows] + gate_ref[0] * jnp.dot(act_s[s], wdn_ref[...], preferred_element_type=F32)


def _ffn_call(x1, mods3, g_ffn, w_up, conv_w, conv_b, w_down):
    b, t, d = x1.shape
    tm = FFN_TILE
    hb = tm // HALO
    n_halo_blocks = t // HALO
    tok = pl.BlockSpec((None, tm, d), lambda bi, i: (bi, i, 0))
    prev = pl.BlockSpec((None, HALO, d), lambda bi, i: (bi, jnp.maximum(i * hb - 1, 0), 0))
    nxt = pl.BlockSpec((None, HALO, d), lambda bi, i: (bi, jnp.minimum((i + 1) * hb, n_halo_blocks - 1), 0))
    mod = lambda k: pl.BlockSpec((1, 1, D_MODEL), lambda bi, i: (bi * N_MOD + k, 0, 0))
    const = lambda shape: pl.BlockSpec(shape, lambda bi, i: (0,) * len(shape),
                                       pipeline_mode=pl.Buffered(1))
    bufs = [((tm, d), F32, 4), ((D_MODEL, 2 * D_FF), BF16, 1), ((D_FF, D_MODEL), BF16, 1),
            ((tm, D_FF), BF16, 1), ((tm + 2 * HALO, d), F32, 2)]
    return pl.pallas_call(
        _ffn_kernel,
        grid=(b, t // tm),
        in_specs=[tok, prev, nxt, mod(3), mod(4), mod(5), const((1, D_MODEL)),
                  const((D_MODEL, 2 * D_FF)), const((FFN_CONV_W, 2 * D_FF)), const((1, 2 * D_FF)),
                  const((D_FF, D_MODEL))],
        out_specs=tok,
        out_shape=jax.ShapeDtypeStruct((b, t, d), F32),
        scratch_shapes=[pltpu.VMEM((FFN_SUBTILES, tm // FFN_SUBTILES, D_FF), BF16)],
        compiler_params=_params(("arbitrary", "arbitrary"), bufs),
        name="conv_ffn",
    )(x1, x1, x1, mods3, mods3, mods3, g_ffn, w_up, conv_w, conv_b, w_down)


def _rope_tables(t):
    n = HEAD_DIM // 4
    freq = (ROPE_BASE ** (-np.arange(n, dtype=np.float32) / n)).astype(np.float32)
    pos = np.arange(t)
    ang_r = (pos // GRID_W).astype(np.float32)[:, None] * freq
    ang_c = (pos % GRID_W).astype(np.float32)[:, None] * freq
    cos = np.concatenate([np.cos(ang_r)] * 2 + [np.cos(ang_c)] * 2, axis=-1)
    sin = np.concatenate([-np.sin(ang_r), np.sin(ang_r), -np.sin(ang_c), np.sin(ang_c)], axis=-1)
    reps = LANES // HEAD_DIM
    return (jnp.asarray(np.tile(cos, (1, reps)), F32), jnp.asarray(np.tile(sin, (1, reps)), F32))


def kernel(x, c, ctx, c_ctx, w_mod, b_mod, norm_mix_g, norm_ffn_g, w_in, lru_conv_w, lru_conv_b, lru_wa, lru_ba, lru_wx, lru_bx, lru_lambda, q_norm_g, k_norm_g, na_rpb, w_rnn_out, w_na_out, w_out, w_up, ffn_conv_w, ffn_conv_b, w_down):
    depth = w_mod.shape[0]
    assert depth == 1, "single-layer block"
    b, t, d = x.shape
    assert d == D_MODEL and b < MOD_ROWS and t % TOK_TILE == 0 and t % GRID_W == 0
    l = 0

    cc = jnp.zeros((MOD_ROWS, d), F32).at[:b].set(c).at[b].set(c_ctx)
    w_in_b = w_in[l].astype(BF16)
    g_mix = norm_mix_g[l][None, :]
    g_ffn = norm_ffn_g[l][None, :]
    q_gain = jnp.tile(q_norm_g[l], NA_HEADS)[None, :]
    k_gain = jnp.tile(k_norm_g[l], NA_HEADS)[None, :]
    head_of = np.arange(MXU_DIM) // HEAD_DIM
    e_mat = jnp.asarray(np.where(head_of[:, None] == head_of[None, :], 1.0 / HEAD_DIM, 0.0), BF16)
    cos_t, sin_t = _rope_tables(t)
    wcat = jnp.concatenate([lru_wa[l], lru_wx[l]], axis=-1).astype(BF16)
    bcat = jnp.stack([lru_ba[l, 0], lru_bx[l, 0], lru_ba[l, 1], lru_bx[l, 1]])
    bias_tab = _bias_table(na_rpb[l])

    mods = _mod_call(cc, w_mod[l], b_mod[l][None, :])
    mods3 = mods.reshape(MOD_ROWS * N_MOD, 1, d)

    k_rot, v, q_rot, q_plain = _qkv_call(x, mods3, 0, g_mix, w_in_b, k_gain, q_gain, e_mat,
                                         cos_t, sin_t, latent=True)
    kc, vc = _qkv_call(ctx.reshape(1, -1, d), mods3, b, g_mix, w_in_b, k_gain, None, e_mat, None, None,
                       latent=False)
    kc, vc = kc.reshape(b, -1, D_ATT), vc.reshape(b, -1, D_ATT)
    y_rnn = _lru_call(x, ctx, mods3, g_mix, w_in_b, lru_conv_w[l], lru_conv_b[l][None, :],
                      wcat, bcat, lru_lambda[l])
    y_na = _attn_call(q_rot, q_plain, k_rot, v, kc, vc, bias_tab)
    x1 = _merge_call(x, y_rnn, y_na, mods3, g_mix, w_in_b, w_rnn_out[l].astype(BF16),
                     w_na_out[l].astype(BF16), w_out[l].astype(BF16))
    return _ffn_call(x1, mods3, g_ffn, w_up[l].astype(BF16), ffn_conv_w[l], ffn_conv_b[l][None, :],
                     w_down[l].astype(BF16))
```

```python
import functools

import jax
import jax.numpy as jnp
import numpy as np
from jax import lax
from jax.experimental import pallas as pl
from jax.experimental.pallas import tpu as pltpu

F32 = jnp.float32
BF16 = jnp.bfloat16

D_MODEL = 1024
GRID_W = 64
D_RNN = D_MODEL
LRU_BLOCKS = 8
LRU_BLOCK_W = D_RNN // LRU_BLOCKS
LRU_CONV_W = 4
LRU_CONV_LEFT = 2
LRU_C = 8.0
NA_HEADS = 16
HEAD_DIM = 64
D_ATT = NA_HEADS * HEAD_DIM
NA_ROWS = 8
NA_COLS = 16
ROPE_BASE = 10000.0
D_FF = ((8 * D_MODEL // 3 + 127) // 128) * 128
FFN_CONV_W = 3
FFN_CONV_LEFT = 1
N_MOD = 6
EPS = 1e-6
NEG_INF = -1e30

COL_XR, COL_K, COL_V, COL_GX, COL_Q, COL_MR, COL_MN = range(7)

LANES = 128
SUBLANES = 8
MXU_DIM = 256
VMEM_LIMIT_CAP = 60000 * 1024
VMEM_TEMP_HEADROOM = 12 * 1024 * 1024

MOD_ROWS = 16
TOK_TILE = 512
LRU_CG = 256
LRU_CHUNK = 128
HALO = SUBLANES
FF_CHUNK = MXU_DIM
FFN_TILE = 512
FFN_SUBTILES = 2
QKV_SUBTILES = 2
PERM_ROWS = 128
ATTN_GROUP = 4


def _nbytes(shape, dtype):
    n = 1
    for s in shape:
        n *= s
    return n * jnp.dtype(dtype).itemsize


def _vmem_limit(buffers):
    need = sum(_nbytes(s, d) * c for s, d, c in buffers) + VMEM_TEMP_HEADROOM
    return int(min(need, VMEM_LIMIT_CAP))


def _params(semantics, buffers):
    return pltpu.CompilerParams(dimension_semantics=semantics,
                                vmem_limit_bytes=_vmem_limit(buffers))


def _modulate(x, g, shift, scale):
    ms = jnp.mean(x * x, axis=-1, keepdims=True)
    y = x * lax.rsqrt(ms + EPS)
    return (y * g) * (1.0 + scale) + shift


def _mod_kernel(cc_ref, w_ref, b_ref, o_ref):
    s = cc_ref[...]
    s = s * jax.nn.sigmoid(s)
    o_ref[...] = jnp.dot(s.astype(BF16), w_ref[...].astype(BF16),
                         preferred_element_type=F32) + b_ref[...]


def _mod_call(cc, w_mod, b_mod):
    n = w_mod.shape[1]
    tn = D_MODEL
    bufs = [((MOD_ROWS, D_MODEL), F32, 2), ((D_MODEL, tn), F32, 2), ((MOD_ROWS, tn), F32, 2)]
    return pl.pallas_call(
        _mod_kernel,
        grid=(n // tn,),
        in_specs=[pl.BlockSpec((MOD_ROWS, D_MODEL), lambda j: (0, 0)),
                  pl.BlockSpec((D_MODEL, tn), lambda j: (0, j)),
                  pl.BlockSpec((1, tn), lambda j: (0, j))],
        out_specs=pl.BlockSpec((MOD_ROWS, tn), lambda j: (0, j)),
        out_shape=jax.ShapeDtypeStruct((MOD_ROWS, n), F32),
        compiler_params=_params(("arbitrary",), bufs),
        name="adaln_mod",
    )(cc, w_mod, b_mod)


def _head_rms(t, gain, e):
    sq = t * t
    hi = sq.astype(BF16)
    lo = (sq - hi.astype(F32)).astype(BF16)
    ms = jnp.dot(hi, e, preferred_element_type=F32) + jnp.dot(lo, e, preferred_element_type=F32)
    return (t * lax.rsqrt(ms + EPS)) * gain


def _rope(y, cos, sin_signed, first_half):
    partner = jnp.where(first_half, pltpu.roll(y, LANES - 16, 1), pltpu.roll(y, 16, 1))
    return y * cos + partner * sin_signed


def _qkv_kernel(*refs, latent):
    if latent:
        (x_ref, sh_ref, sc_ref, g_ref, wk_ref, wv_ref, wq_ref, kg_ref, qg_ref, e_ref,
         cos_ref, sin_ref, k_out, v_out, qr_out, qp_out) = refs
    else:
        (x_ref, sh_ref, sc_ref, g_ref, wk_ref, wv_ref, kg_ref, e_ref, k_out, v_out) = refs
    e = e_ref[...]
    scale = HEAD_DIM ** -0.5
    tm = x_ref.shape[0]
    ts = tm // QKV_SUBTILES
    for s in range(QKV_SUBTILES):
        rows = slice(s * ts, (s + 1) * ts)
        xn = _modulate(x_ref[rows, :], g_ref[...], sh_ref[0], sc_ref[0]).astype(BF16)
        if latent:
            cos = cos_ref[rows, :]
            sin = sin_ref[rows, :]
            lane = lax.broadcasted_iota(jnp.int32, cos.shape, 1)
            first_half = (lane % 32) < 16
        for j in range(D_ATT // MXU_DIM):
            sl = slice(MXU_DIM * j, MXU_DIM * (j + 1))
            v_out[rows, sl] = jnp.dot(xn, wv_ref[:, sl], preferred_element_type=F32).astype(BF16)
            k = _head_rms(jnp.dot(xn, wk_ref[:, sl], preferred_element_type=F32), kg_ref[:, sl], e)
            if not latent:
                k_out[rows, sl] = k.astype(BF16)
                continue
            q = _head_rms(jnp.dot(xn, wq_ref[:, sl], preferred_element_type=F32), qg_ref[:, sl], e)
            qp_out[rows, sl] = (q * scale).astype(BF16)
            for h in range(MXU_DIM // LANES):
                hl = slice(LANES * h, LANES * (h + 1))
                ol = slice(MXU_DIM * j + LANES * h, MXU_DIM * j + LANES * (h + 1))
                k_out[rows, ol] = _rope(k[:, hl], cos, sin, first_half).astype(BF16)
                qr_out[rows, ol] = (_rope(q[:, hl], cos, sin, first_half) * scale).astype(BF16)


def _qkv_call(x, mods3, mod_row0, g_mix, w_in, k_gain, q_gain, e_mat, cos_t, sin_t, latent):
    b, t, d = x.shape
    tm = min(TOK_TILE, t)
    nt = t // tm
    wblk = lambda c: pl.BlockSpec((D_MODEL, D_MODEL), lambda bi, i: (0, c))
    vec = pl.BlockSpec((1, D_MODEL), lambda bi, i: (0, 0))
    if latent:
        mod = lambda k: pl.BlockSpec((1, 1, D_MODEL), lambda bi, i: (bi * N_MOD + k, 0, 0))
    else:
        mod = lambda k: pl.BlockSpec((1, 1, D_MODEL), lambda bi, i: (mod_row0 * N_MOD + k, 0, 0))
    tok = pl.BlockSpec((None, tm, d), lambda bi, i: (bi, i, 0))
    in_specs = [tok, mod(0), mod(1), vec, wblk(COL_K), wblk(COL_V)]
    args = [x, mods3, mods3, g_mix, w_in, w_in]
    if latent:
        in_specs += [wblk(COL_Q), vec, vec]
        args += [w_in, k_gain, q_gain]
    else:
        in_specs += [vec]
        args += [k_gain]
    in_specs += [pl.BlockSpec((MXU_DIM, MXU_DIM), lambda bi, i: (0, 0))]
    args += [e_mat]
    n_out = 2
    if latent:
        rope_spec = pl.BlockSpec((tm, LANES), lambda bi, i: (i, 0))
        in_specs += [rope_spec, rope_spec]
        args += [cos_t, sin_t]
        n_out = 4
    out_sd = jax.ShapeDtypeStruct((b, t, D_ATT), BF16)
    bufs = [((tm, d), F32, 2), ((D_MODEL, D_MODEL), BF16, 6), ((tm, D_ATT), BF16, 2 * n_out),
            ((tm, LANES), F32, 4), ((tm, d), BF16, 1)]
    return pl.pallas_call(
        functools.partial(_qkv_kernel, latent=latent),
        grid=(b, nt),
        in_specs=in_specs,
        out_specs=[tok] * n_out,
        out_shape=[out_sd] * n_out,
        compiler_params=_params(("arbitrary", "arbitrary"), bufs),
        name="qkv_proj" if latent else "ctx_proj",
    )(*args)


def _softplus(x):
    return jnp.maximum(x, 0.0) + jnp.log1p(jnp.exp(-jnp.abs(x)))


SEG_BLOCK = SUBLANES * SUBLANES
LRU_HALO_FRONT = 2 * SUBLANES
LRU_HALO_BACK = SUBLANES


def _seg_permutation(n):
    p = np.arange(n)
    src = (p // SEG_BLOCK) * SEG_BLOCK + (p % SUBLANES) * SUBLANES + (p % SEG_BLOCK) // SUBLANES
    return jnp.asarray(src[:, None] == p[None, :], BF16)


def _seg_permute_rows(perm, x):
    n = perm.shape[0]
    parts = [jnp.dot(perm, x[n * i:n * (i + 1)], preferred_element_type=F32) for i in range(x.shape[0] // n)]
    return jnp.concatenate(parts, axis=0).astype(BF16)


def _seg_conv(x, before, after, taps, bias, left):
    n, c = x.shape
    right = len(taps) - 1 - left
    nb = n // SEG_BLOCK
    tiles = [x[SUBLANES * i:SUBLANES * (i + 1)] for i in range(n // SUBLANES)]
    sub = lax.broadcasted_iota(jnp.int32, (SUBLANES, c), 0)
    out = []
    for blk in range(nb):
        cur = tiles[SUBLANES * blk:SUBLANES * (blk + 1)]
        down, up = {}, {}
        for e in range(1, left + 1):
            j = SUBLANES - e
            prev = tiles[SUBLANES * (blk - 1) + j] if blk > 0 else before[left - e]
            down[j] = jnp.where(sub == 0, pltpu.roll(prev, 1, 0), pltpu.roll(cur[j], 1, 0))
        for e in range(right):
            nxt = tiles[SUBLANES * (blk + 1) + e] if blk + 1 < nb else after[e]
            up[e] = jnp.where(sub == SUBLANES - 1, pltpu.roll(nxt, SUBLANES - 1, 0),
                              pltpu.roll(cur[e], SUBLANES - 1, 0))
        for j in range(SUBLANES):
            acc = bias
            for k, wk in enumerate(taps):
                src = j + k - left
                if src < 0:
                    tile = down[src + SUBLANES]
                elif src >= SUBLANES:
                    tile = up[src - SUBLANES]
                else:
                    tile = cur[src]
                acc = acc + wk * tile
            out.append(acc)
    return jnp.concatenate(out, axis=0)


def _lru_coeffs(xc, wcat_ref, bcat, sp8, d):
    xcb = xc.astype(BF16)
    rs, gs = [], []
    for blk in range(LRU_CG // LRU_BLOCK_W):
        bl = slice(LRU_BLOCK_W * blk, LRU_BLOCK_W * (blk + 1))
        z = jnp.dot(xcb[:, bl], wcat_ref[d, blk], preferred_element_type=F32)
        rs.append(z[:, :LRU_BLOCK_W])
        gs.append(z[:, LRU_BLOCK_W:])
    r = jax.nn.sigmoid(jnp.concatenate(rs, axis=-1) + bcat[2 * d:2 * d + 1, :])
    gate = jax.nn.sigmoid(jnp.concatenate(gs, axis=-1) + bcat[2 * d + 1:2 * d + 2, :])
    a = jnp.exp(r * -sp8[d:d + 1, :])
    w = jnp.tanh(r * sp8[d:d + 1, :]) * (a * a + 1.0)
    root = jnp.where(w > 0.0, w * lax.rsqrt(w), 0.0)
    b = root * gate * xc
    return a, b


def _seg_scan(a, b, carry, reverse, want_h=True):
    n, c = a.shape
    nb = n // SEG_BLOCK
    sub = lax.broadcasted_iota(jnp.int32, (SUBLANES, c), 0)
    steps = range(SUBLANES - 1, -1, -1) if reverse else range(SUBLANES)
    hs = [None] * (nb * SUBLANES)
    for blk in (range(nb - 1, -1, -1) if reverse else range(nb)):
        tile = lambda x, j: x[blk * SEG_BLOCK + j * SUBLANES:blk * SEG_BLOCK + (j + 1) * SUBLANES]
        acc_a, acc_h = {}, {}
        prev = None
        for j in steps:
            aj, bj = tile(a, j), tile(b, j)
            acc_h[j] = bj if prev is None else aj * acc_h[prev] + bj
            acc_a[j] = aj if prev is None else aj * acc_a[prev]
            prev = j
        pa, ph = acc_a[prev], acc_h[prev]
        for s in (1, 2, 4):
            shift = SUBLANES - s if reverse else s
            m = (sub < SUBLANES - s) if reverse else (sub >= s)
            ph = jnp.where(m, pa * pltpu.roll(ph, shift, 0) + ph, ph)
            pa = jnp.where(m, pa * pltpu.roll(pa, shift, 0), pa)
        end_state = pa * carry + ph
        if reverse:
            seg_in = jnp.where(sub == SUBLANES - 1, carry, pltpu.roll(end_state, SUBLANES - 1, 0))
            carry = end_state[0:1]
        else:
            seg_in = jnp.where(sub == 0, carry, pltpu.roll(end_state, 1, 0))
            carry = end_state[SUBLANES - 1:SUBLANES]
        if want_h:
            for j in steps:
                hs[blk * SUBLANES + j] = acc_h[j] + acc_a[j] * seg_in
    return (jnp.concatenate(hs, axis=0) if want_h else None), carry


def _lru_kernel(x_ref, ctx_ref, shx_ref, scx_ref, shc_ref, scc_ref, g_ref, perm_ref, wxr_ref, wgx_ref,
                cw_ref, cb_ref, wcat_ref, bcat_ref, lam_ref, y_ref,
                xn_s, cn_s, xr_s, cr_s, xc_s, hf_s):
    t = x_ref.shape[0]
    nctx = ctx_ref.shape[0]
    cg = y_ref.shape[1]
    perm = perm_ref[...]

    def permuted_norm(rows_f32, shift, scale):
        return _seg_permute_rows(perm, _modulate(rows_f32, g_ref[...], shift, scale).astype(BF16))

    @pl.when(pl.program_id(1) == 0)
    def _():
        def body(i, _):
            r0 = pl.multiple_of(i * TOK_TILE, TOK_TILE)
            xn_s[pl.ds(r0, TOK_TILE), :] = permuted_norm(x_ref[pl.ds(r0, TOK_TILE), :], shx_ref[0], scx_ref[0])
            return 0
        lax.fori_loop(0, t // TOK_TILE, body, 0)
        cn_s[...] = permuted_norm(ctx_ref[...], shc_ref[0], scc_ref[0])

    xr_s[0:LRU_HALO_FRONT, :] = jnp.zeros((LRU_HALO_FRONT, cg), F32)
    xr_s[t + LRU_HALO_FRONT:, :] = jnp.zeros((LRU_HALO_BACK, cg), F32)
    cr_s[0:LRU_HALO_FRONT, :] = jnp.zeros((LRU_HALO_FRONT, cg), F32)
    cr_s[nctx + LRU_HALO_FRONT:, :] = jnp.zeros((LRU_HALO_BACK, cg), F32)

    def project_xr(blk):
        rows = slice(blk * TOK_TILE, (blk + 1) * TOK_TILE)
        xr_s[LRU_HALO_FRONT + blk * TOK_TILE:LRU_HALO_FRONT + (blk + 1) * TOK_TILE, :] = jnp.dot(
            xn_s[rows, :], wxr_ref[...], preferred_element_type=F32)
    cr_s[LRU_HALO_FRONT:LRU_HALO_FRONT + nctx, :] = jnp.dot(cn_s[...], wxr_ref[...],
                                                            preferred_element_type=F32)
    project_xr(0)

    conv_w = cw_ref[...]
    taps = [conv_w[k:k + 1, :] for k in range(LRU_CONV_W)]
    conv_b = cb_ref[...]
    bcat = bcat_ref[...]
    sp8 = LRU_C * _softplus(-lam_ref[...])
    chunk_ext = LRU_CHUNK + LRU_HALO_FRONT + LRU_HALO_BACK

    def conv(xe):
        before = [xe[SUBLANES * i:SUBLANES * (i + 1)] for i in range(LRU_HALO_FRONT // SUBLANES)]
        after = [xe[LRU_HALO_FRONT + LRU_CHUNK:]]
        return _seg_conv(xe[LRU_HALO_FRONT:LRU_HALO_FRONT + LRU_CHUNK], before, after, taps, conv_b,
                         LRU_CONV_LEFT)
    coeffs = functools.partial(_lru_coeffs, wcat_ref=wcat_ref, bcat=bcat, sp8=sp8)
    n_ctx = nctx // LRU_CHUNK
    ctx_xc = [conv(cr_s[i * LRU_CHUNK:i * LRU_CHUNK + chunk_ext, :]) for i in range(n_ctx)]

    carry = jnp.zeros((1, cg), F32)
    for i in range(n_ctx):
        a, b = coeffs(ctx_xc[i], d=0)
        _, carry = _seg_scan(a, b, carry, reverse=False, want_h=False)

    chunks_per_blk = TOK_TILE // LRU_CHUNK
    n_blk = t // TOK_TILE
    for blk in range(n_blk):
        if blk + 1 < n_blk:
            project_xr(blk + 1)
        for c in range(chunks_per_blk):
            r0 = (blk * chunks_per_blk + c) * LRU_CHUNK
            xc = conv(xr_s[r0:r0 + chunk_ext, :])
            xc_s[r0:r0 + LRU_CHUNK, :] = xc
            a, b = coeffs(xc, d=0)
            h, carry = _seg_scan(a, b, carry, reverse=False)
            hf_s[r0:r0 + LRU_CHUNK, :] = h

    carry = jnp.zeros((1, cg), F32)
    for i in range(n_ctx - 1, -1, -1):
        a, b = coeffs(ctx_xc[i], d=1)
        _, carry = _seg_scan(a, b, carry, reverse=True, want_h=False)

    def store_unpermuted(blk, y_perm):
        y_ref[blk * TOK_TILE:(blk + 1) * TOK_TILE, :] = _seg_permute_rows(perm, y_perm)

    pending = None
    for blk in range(n_blk - 1, -1, -1):
        gate = jax.nn.gelu(jnp.dot(xn_s[blk * TOK_TILE:(blk + 1) * TOK_TILE, :], wgx_ref[...],
                                   preferred_element_type=F32))
        ys = [None] * chunks_per_blk
        for c in range(chunks_per_blk - 1, -1, -1):
            r0 = (blk * chunks_per_blk + c) * LRU_CHUNK
            a, b = coeffs(xc_s[r0:r0 + LRU_CHUNK, :], d=1)
            h, carry = _seg_scan(a, b, carry, reverse=True)
            ys[c] = ((hf_s[r0:r0 + LRU_CHUNK, :] + h)
                     * gate[c * LRU_CHUNK:(c + 1) * LRU_CHUNK]).astype(BF16)
        if pending is not None:
            store_unpermuted(*pending)
        pending = (blk, jnp.concatenate(ys, axis=0))
    store_unpermuted(*pending)


def _lru_call(x, ctx, mods3, g_mix, w_in, conv_w, conv_b, wcat, bcat, lam):
    b, t, d = x.shape
    nctx = ctx.shape[1]
    cg = LRU_CG
    ng = D_RNN // cg
    ctx_row = b * N_MOD
    mod = lambda k: pl.BlockSpec((1, 1, D_MODEL), lambda bi, g: (bi * N_MOD + k, 0, 0))
    modc = lambda k: pl.BlockSpec((1, 1, D_MODEL), lambda bi, g: (ctx_row + k, 0, 0))
    in_specs = [
        pl.BlockSpec((None, t, d), lambda bi, g: (bi, 0, 0)),
        pl.BlockSpec((None, nctx, d), lambda bi, g: (bi, 0, 0)),
        mod(0), mod(1), modc(0), modc(1),
        pl.BlockSpec((1, D_MODEL), lambda bi, g: (0, 0)),
        pl.BlockSpec((PERM_ROWS, PERM_ROWS), lambda bi, g: (0, 0)),
        pl.BlockSpec((D_MODEL, cg), lambda bi, g: (0, COL_XR * ng + g)),
        pl.BlockSpec((D_MODEL, cg), lambda bi, g: (0, COL_GX * ng + g)),
        pl.BlockSpec((LRU_CONV_W, cg), lambda bi, g: (0, g)),
        pl.BlockSpec((1, cg), lambda bi, g: (0, g)),
        pl.BlockSpec((2, cg // LRU_BLOCK_W, LRU_BLOCK_W, 2 * LRU_BLOCK_W), lambda bi, g: (0, g, 0, 0)),
        pl.BlockSpec((4, cg), lambda bi, g: (0, g)),
        pl.BlockSpec((2, cg), lambda bi, g: (0, g)),
    ]
    halo = LRU_HALO_FRONT + LRU_HALO_BACK
    scratch = [
        pltpu.VMEM((t, d), BF16), pltpu.VMEM((nctx, d), BF16),
        pltpu.VMEM((t + halo, cg), F32), pltpu.VMEM((nctx + halo, cg), F32),
        pltpu.VMEM((t, cg), F32), pltpu.VMEM((t, cg), F32),
    ]
    bufs = [((t, d), F32, 2), ((nctx, d), F32, 2), ((D_MODEL, cg), BF16, 4), ((t, cg), BF16, 2),
            ((t, d), BF16, 1), ((nctx, d), BF16, 1), ((t + halo, cg), F32, 3),
            ((nctx + halo, cg), F32, 1)]
    return pl.pallas_call(
        _lru_kernel,
        grid=(b, ng),
        in_specs=in_specs,
        out_specs=pl.BlockSpec((None, t, cg), lambda bi, g: (bi, 0, g)),
        out_shape=jax.ShapeDtypeStruct((b, t, D_RNN), BF16),
        scratch_shapes=scratch,
        compiler_params=_params(("arbitrary", "arbitrary"), bufs),
        name="rglru",
    )(x, ctx, mods3, mods3, mods3, mods3, g_mix, _seg_permutation(PERM_ROWS), w_in, w_in,
      conv_w, conv_b, wcat, bcat, lam)


NT_DIMS = (((1,), (1,)), ((), ()))
KEY_ROWS_PER_TILE = LANES // GRID_W


def _attn_group_geometry(g, rows):
    r0 = g * ATTN_GROUP
    starts = [min(max(r0 + o - NA_ROWS // 2, 0), rows - NA_ROWS) for o in range(ATTN_GROUP)]
    key_row0 = starts[0]
    n_key_rows = -(-(starts[-1] + NA_ROWS - key_row0) // KEY_ROWS_PER_TILE) * KEY_ROWS_PER_TILE
    desc = tuple((starts[o] - key_row0, key_row0 - (r0 + o)) for o in range(ATTN_GROUP))
    return key_row0, n_key_rows, desc


BIAS_BOTH, BIAS_RIGHT_ONLY, BIAS_LEFT_ONLY = range(3)


def _bias_tile_index(off, base, tile):
    left = KEY_ROWS_PER_TILE * tile
    in_window = lambda j: off <= j < off + NA_ROWS
    dr_left = base + left + (NA_ROWS - 1)
    if in_window(left) and in_window(left + 1):
        return BIAS_BOTH, dr_left
    if in_window(left + 1):
        return BIAS_RIGHT_ONLY, dr_left + 1
    return BIAS_LEFT_ONLY, dr_left


N_ROW_OFFSETS = 2 * NA_ROWS - 1


def _bias_edge_tiles(rows):
    edges = []
    for g in range(rows // ATTN_GROUP):
        _, _, desc = _attn_group_geometry(g, rows)
        for off, base in desc:
            for tile in range(off // KEY_ROWS_PER_TILE, -(-(off + NA_ROWS) // KEY_ROWS_PER_TILE)):
                entry = _bias_tile_index(off, base, tile)
                if entry[0] != BIAS_BOTH and entry not in edges:
                    edges.append(entry)
    return edges


def _attn_kernel(qr_ref, qp_ref, k_ref, v_ref, kc_ref, vc_ref, bias_ref, y_ref, qs_s, qps_s):
    t = k_ref.shape[0]
    rows = t // GRID_W
    lane = lax.broadcasted_iota(jnp.int32, (rows, GRID_W, LANES), 2)
    head_a = lane < HEAD_DIM

    def stack_heads(q_ref, dst):
        q3 = q_ref[...].reshape(rows, GRID_W, LANES)
        zero = jnp.zeros_like(q3)
        dst[...] = jnp.concatenate([jnp.where(head_a, q3, zero), jnp.where(head_a, zero, q3)], axis=1)
    stack_heads(qr_ref, qs_s)
    stack_heads(qp_ref, qps_s)

    kc = kc_ref[...]
    vc = vc_ref[...]
    out_a = lax.broadcasted_iota(jnp.int32, (GRID_W, LANES), 1) < HEAD_DIM
    rows_stacked = 2 * GRID_W
    edge_tiles = _bias_edge_tiles(rows)

    def lane_tiles(arrays):
        return [a[:, LANES * i:LANES * (i + 1)] for a in arrays for i in range(a.shape[1] // LANES)]

    for g in range(rows // ATTN_GROUP):
        r0 = g * ATTN_GROUP
        key_row0, n_key_rows, desc = _attn_group_geometry(g, rows)
        k_sup = k_ref[key_row0 * GRID_W:(key_row0 + n_key_rows) * GRID_W, :]
        v_sup = v_ref[key_row0 * GRID_W:(key_row0 + n_key_rows) * GRID_W, :]
        n_tiles = n_key_rows // KEY_ROWS_PER_TILE
        q_grp = qs_s[r0:r0 + ATTN_GROUP].reshape(ATTN_GROUP * rows_stacked, LANES)
        qp_grp = qps_s[r0:r0 + ATTN_GROUP].reshape(ATTN_GROUP * rows_stacked, LANES)
        s_lat = lax.dot_general(q_grp, k_sup, NT_DIMS, preferred_element_type=F32)
        s_ctx = lax.dot_general(qp_grp, kc, NT_DIMS, preferred_element_type=F32)
        p_rows, pc_rows, denoms = [], [], []
        for o in range(ATTN_GROUP):
            off, base = desc[o]
            lo, hi = off // KEY_ROWS_PER_TILE, -(-(off + NA_ROWS) // KEY_ROWS_PER_TILE)
            rsl = slice(o * rows_stacked, (o + 1) * rows_stacked)
            bias = []
            for tile in range(lo, hi):
                kind, dr = _bias_tile_index(off, base, tile)
                entry = dr if kind == BIAS_BOTH else N_ROW_OFFSETS + edge_tiles.index((kind, dr))
                bias.append(jnp.concatenate([bias_ref[head, entry] for head in range(LANES // HEAD_DIM)],
                                            axis=0))
            s_o = s_lat[rsl, lo * LANES:hi * LANES] + jnp.concatenate(bias, axis=-1)
            sc_o = s_ctx[rsl]
            m = jnp.max(functools.reduce(jnp.maximum, lane_tiles([s_o, sc_o])), axis=-1, keepdims=True)
            p_o = jnp.exp(s_o - m)
            pc_o = jnp.exp(sc_o - m)
            denoms.append(jnp.sum(functools.reduce(jnp.add, lane_tiles([p_o, pc_o])), axis=-1, keepdims=True))
            pieces = [p_o.astype(BF16)]
            if lo > 0:
                pieces.insert(0, jnp.zeros((rows_stacked, lo * LANES), BF16))
            if hi < n_tiles:
                pieces.append(jnp.zeros((rows_stacked, (n_tiles - hi) * LANES), BF16))
            p_rows.append(jnp.concatenate(pieces, axis=-1) if len(pieces) > 1 else pieces[0])
            pc_rows.append(pc_o.astype(BF16))
        o_grp = (jnp.dot(jnp.concatenate(p_rows, axis=0), v_sup, preferred_element_type=F32)
                 + jnp.dot(jnp.concatenate(pc_rows, axis=0), vc, preferred_element_type=F32))
        for o in range(ATTN_GROUP):
            blk = o_grp[o * rows_stacked:(o + 1) * rows_stacked] / denoms[o]
            y_ref[(r0 + o) * GRID_W:(r0 + o + 1) * GRID_W, :] = jnp.where(
                out_a, blk[:GRID_W], blk[GRID_W:]).astype(BF16)


def _attn_call(q_rot, q_plain, k_rot, v, kc, vc, bias_tab):
    b, t, _ = q_rot.shape
    nctx = kc.shape[1]
    rows = t // GRID_W
    npair = D_ATT // LANES
    tok = pl.BlockSpec((None, t, LANES), lambda bi, hp: (bi, 0, hp))
    ctx = pl.BlockSpec((None, nctx, LANES), lambda bi, hp: (bi, 0, hp))
    bias_shape = (LANES // HEAD_DIM,) + bias_tab.shape[1:]
    bufs = [((t, LANES), BF16, 10), ((nctx, LANES), BF16, 4), (bias_shape, F32, 2),
            ((rows, 2 * GRID_W, LANES), BF16, 2)]
    return pl.pallas_call(
        _attn_kernel,
        grid=(b, npair),
        in_specs=[tok, tok, tok, tok, ctx, ctx,
                  pl.BlockSpec(bias_shape, lambda bi, hp: (hp, 0, 0, 0))],
        out_specs=tok,
        out_shape=jax.ShapeDtypeStruct((b, t, D_ATT), BF16),
        scratch_shapes=[pltpu.VMEM((rows, 2 * GRID_W, LANES), BF16),
                        pltpu.VMEM((rows, 2 * GRID_W, LANES), BF16)],
        compiler_params=_params(("arbitrary", "arbitrary"), bufs),
        name="nbr_attn",
    )(q_rot, q_plain, k_rot, v, kc, vc, bias_tab)


def _bias_table(rpb, rows):
    c = np.arange(GRID_W)
    col_start = np.clip(c - NA_COLS // 2, 0, GRID_W - NA_COLS)
    in_win = (c[None, :] >= col_start[:, None]) & (c[None, :] < col_start[:, None] + NA_COLS)
    dc = np.clip(c[None, :] - c[:, None], -(NA_COLS - 1), NA_COLS - 1) + (NA_COLS - 1)
    select = (dc[None] == np.arange(2 * NA_COLS - 1)[:, None, None]).astype(np.float32)
    toe = jnp.einsum("hdm,mqk->hdqk", rpb, jnp.asarray(select), precision=lax.Precision.HIGHEST)
    toe = jnp.where(jnp.asarray(in_win), toe, NEG_INF)
    neg = jnp.full_like(toe[:, :1], NEG_INF)
    nxt = jnp.concatenate([toe[:, 1:], neg], axis=1)
    entries = [jnp.concatenate([toe, nxt], axis=-1)]
    for kind, dr in _bias_edge_tiles(rows):
        halves = [neg, toe[:, dr:dr + 1]] if kind == BIAS_RIGHT_ONLY else [toe[:, dr:dr + 1], neg]
        entries.append(jnp.concatenate(halves, axis=-1))
    return jnp.concatenate(entries, axis=1)


def _merge_kernel(x_ref, yr_ref, yn_ref, sh_ref, sc_ref, gate_ref, g_ref,
                  wgr_ref, wgn_ref, wr_ref, wn_ref, wo_ref, o_ref):
    x = x_ref[...]
    xn = _modulate(x, g_ref[...], sh_ref[0], sc_ref[0]).astype(BF16)
    z_r = jnp.dot(yr_ref[...], wr_ref[...], preferred_element_type=F32)
    z_n = jnp.dot(yn_ref[...], wn_ref[...], preferred_element_type=F32)
    g_r = jax.nn.sigmoid(jnp.dot(xn, wgr_ref[...], preferred_element_type=F32))
    g_n = jax.nn.sigmoid(jnp.dot(xn, wgn_ref[...], preferred_element_type=F32))
    merged = g_r * z_r + g_n * z_n
    o_ref[...] = x + gate_ref[0] * jnp.dot(merged.astype(BF16), wo_ref[...], preferred_element_type=F32)


def _merge_call(x, y_rnn, y_na, mods3, g_mix, w_in, w_rnn_out, w_na_out, w_out):
    b, t, d = x.shape
    tm = TOK_TILE
    tok = pl.BlockSpec((None, tm, d), lambda bi, i: (bi, i, 0))
    mod = lambda k: pl.BlockSpec((1, 1, D_MODEL), lambda bi, i: (bi * N_MOD + k, 0, 0))
    wblk = lambda c: pl.BlockSpec((D_MODEL, D_MODEL), lambda bi, i: (0, c))
    bufs = [((tm, d), F32, 4), ((tm, d), BF16, 4), ((D_MODEL, D_MODEL), BF16, 10), ((tm, d), F32, 4)]
    return pl.pallas_call(
        _merge_kernel,
        grid=(b, t // tm),
        in_specs=[tok, tok, tok, mod(0), mod(1), mod(2),
                  pl.BlockSpec((1, D_MODEL), lambda bi, i: (0, 0)),
                  wblk(COL_MR), wblk(COL_MN), wblk(0), wblk(0), wblk(0)],
        out_specs=tok,
        out_shape=jax.ShapeDtypeStruct((b, t, d), F32),
        compiler_params=_params(("arbitrary", "arbitrary"), bufs),
        name="merge",
    )(x, y_rnn, y_na, mods3, mods3, mods3, g_mix, w_in, w_in, w_rnn_out, w_na_out, w_out)


def _ffn_kernel(x_ref, xp_ref, xnx_ref, sh_ref, sc_ref, gate_ref, g_ref, wup_ref, cw_ref, cb_ref,
                wdn_ref, o_ref, act_s):
    tm = x_ref.shape[0]
    i = pl.program_id(1)
    x = x_ref[...]
    xe = jnp.concatenate([xp_ref[...], x, xnx_ref[...]], axis=0)
    xn = _modulate(xe, g_ref[...], sh_ref[0], sc_ref[0])
    row = lax.broadcasted_iota(jnp.int32, (tm + 2 * HALO, 1), 0)
    outside = ((row < HALO) & (i == 0)) | ((row >= tm + HALO) & (i == pl.num_programs(1) - 1))
    xn = jnp.where(outside, 0.0, xn).astype(BF16)
    cw = cw_ref[...]
    cb = cb_ref[...]
    ts = tm // FFN_SUBTILES
    for s in range(FFN_SUBTILES):
        xs = xn[s * ts:s * ts + ts + 2 * HALO]
        for c in range(D_FF // FF_CHUNK):
            halves = []
            for half in range(2):
                sl = slice(half * D_FF + c * FF_CHUNK, half * D_FF + (c + 1) * FF_CHUNK)
                u = jnp.dot(xs, wup_ref[:, sl], preferred_element_type=F32)
                acc = cb[:, sl]
                for k in range(FFN_CONV_W):
                    off = HALO - FFN_CONV_LEFT + k
                    acc = acc + cw[k:k + 1, sl] * u[off:off + ts]
                halves.append(acc)
            a, g = halves
            act_s[s, :, c * FF_CHUNK:(c + 1) * FF_CHUNK] = ((a * jax.nn.sigmoid(a)) * g).astype(BF16)
        rows = slice(s * ts, (s + 1) * ts)
        o_ref[rows, :] = x[rows] + gate_ref[0] * jnp.dot(act_s[s], wdn_ref[...], preferred_element_type=F32)


def _ffn_call(x1, mods3, g_ffn, w_up, conv_w, conv_b, w_down):
    b, t, d = x1.shape
    tm = FFN_TILE
    hb = tm // HALO
    n_halo_blocks = t // HALO
    tok = pl.BlockSpec((None, tm, d), lambda bi, i: (bi, i, 0))
    prev = pl.BlockSpec((None, HALO, d), lambda bi, i: (bi, jnp.maximum(i * hb - 1, 0), 0))
    nxt = pl.BlockSpec((None, HALO, d), lambda bi, i: (bi, jnp.minimum((i + 1) * hb, n_halo_blocks - 1), 0))
    mod = lambda k: pl.BlockSpec((1, 1, D_MODEL), lambda bi, i: (bi * N_MOD + k, 0, 0))
    const = lambda shape: pl.BlockSpec(shape, lambda bi, i: (0,) * len(shape),
                                       pipeline_mode=pl.Buffered(1))
    bufs = [((tm, d), F32, 4), ((D_MODEL, 2 * D_FF), BF16, 1), ((D_FF, D_MODEL), BF16, 1),
            ((tm, D_FF), BF16, 1), ((tm + 2 * HALO, d), F32, 2)]
    return pl.pallas_call(
        _ffn_kernel,
        grid=(b, t // tm),
        in_specs=[tok, prev, nxt, mod(3), mod(4), mod(5), const((1, D_MODEL)),
                  const((D_MODEL, 2 * D_FF)), const((FFN_CONV_W, 2 * D_FF)), const((1, 2 * D_FF)),
                  const((D_FF, D_MODEL))],
        out_specs=tok,
        out_shape=jax.ShapeDtypeStruct((b, t, d), F32),
        scratch_shapes=[pltpu.VMEM((FFN_SUBTILES, tm // FFN_SUBTILES, D_FF), BF16)],
        compiler_params=_params(("arbitrary", "arbitrary"), bufs),
        name="conv_ffn",
    )(x1, x1, x1, mods3, mods3, mods3, g_ffn, w_up, conv_w, conv_b, w_down)


def _rope_tables(t):
    n = HEAD_DIM // 4
    freq = (ROPE_BASE ** (-np.arange(n, dtype=np.float32) / n)).astype(np.float32)
    pos = np.arange(t)
    ang_r = (pos // GRID_W).astype(np.float32)[:, None] * freq
    ang_c = (pos % GRID_W).astype(np.float32)[:, None] * freq
    cos = np.concatenate([np.cos(ang_r)] * 2 + [np.cos(ang_c)] * 2, axis=-1)
    sin = np.concatenate([-np.sin(ang_r), np.sin(ang_r), -np.sin(ang_c), np.sin(ang_c)], axis=-1)
    reps = LANES // HEAD_DIM
    return (jnp.asarray(np.tile(cos, (1, reps)), F32), jnp.asarray(np.tile(sin, (1, reps)), F32))


def kernel(x, c, ctx, c_ctx, w_mod, b_mod, norm_mix_g, norm_ffn_g, w_in, lru_conv_w, lru_conv_b, lru_wa, lru_ba, lru_wx, lru_bx, lru_lambda, q_norm_g, k_norm_g, na_rpb, w_rnn_out, w_na_out, w_out, w_up, ffn_conv_w, ffn_conv_b, w_down):
    depth = w_mod.shape[0]
    assert depth == 1, "single-layer block"
    b, t, d = x.shape
    assert d == D_MODEL and b < MOD_ROWS and t % TOK_TILE == 0 and t % GRID_W == 0
    l = 0

    cc = jnp.zeros((MOD_ROWS, d), F32).at[:b].set(c).at[b].set(c_ctx)
    w_in_b = w_in[l].astype(BF16)
    g_mix = norm_mix_g[l][None, :]
    g_ffn = norm_ffn_g[l][None, :]
    q_gain = jnp.tile(q_norm_g[l], NA_HEADS)[None, :]
    k_gain = jnp.tile(k_norm_g[l], NA_HEADS)[None, :]
    head_of = np.arange(MXU_DIM) // HEAD_DIM
    e_mat = jnp.asarray(np.where(head_of[:, None] == head_of[None, :], 1.0 / HEAD_DIM, 0.0), BF16)
    cos_t, sin_t = _rope_tables(t)
    wcat = jnp.concatenate([lru_wa[l], lru_wx[l]], axis=-1).astype(BF16)
    bcat = jnp.stack([lru_ba[l, 0], lru_bx[l, 0], lru_ba[l, 1], lru_bx[l, 1]])
    bias_tab = _bias_table(na_rpb[l], t // GRID_W)

    mods = _mod_call(cc, w_mod[l], b_mod[l][None, :])
    mods3 = mods.reshape(MOD_ROWS * N_MOD, 1, d)

    k_rot, v, q_rot, q_plain = _qkv_call(x, mods3, 0, g_mix, w_in_b, k_gain, q_gain, e_mat,
                                         cos_t, sin_t, latent=True)
    kc, vc = _qkv_call(ctx.reshape(1, -1, d), mods3, b, g_mix, w_in_b, k_gain, None, e_mat, None, None,
                       latent=False)
    kc, vc = kc.reshape(b, -1, D_ATT), vc.reshape(b, -1, D_ATT)
    y_rnn = _lru_call(x, ctx, mods3, g_mix, w_in_b, lru_conv_w[l], lru_conv_b[l][None, :],
                      wcat, bcat, lru_lambda[l])
    y_na = _attn_call(q_rot, q_plain, k_rot, v, kc, vc, bias_tab)
    x1 = _merge_call(x, y_rnn, y_na, mods3, g_mix, w_in_b, w_rnn_out[l].astype(BF16),
                     w_na_out[l].astype(BF16), w_out[l].astype(BF16))
    return _ffn_call(x1, mods3, g_ffn, w_up[l].astype(BF16), ffn_conv_w[l], ffn_conv_b[l][None, :],
                     w_down[l].astype(BF16))
```
